```python
import math
import jax
import jax.numpy as jnp
from jax import lax
import numpy as np

D_MODEL = 2048
BATCH = 4
SEQ = 2048
DEPTH = 4

CHUNK = 64
N_PREV_CHUNKS = 8
BAND_CHUNKS = N_PREV_CHUNKS + 1
Q_BLOCK = 128
ROPE_THETA = 500000.0
PLE_DIM = 256
D_FF = 5632
NORM_EPS = 1e-6
NEG_INF = -1e30

A_HEADS = 8
A_HEAD_DIM = 128
A_WIDTH = A_HEADS * A_HEAD_DIM
REL_FUTURE = CHUNK - 1
REL_PAST_CLIP = 128
REL_TABLE = REL_FUTURE + REL_PAST_CLIP + 1

B_HEADS = 8
B_NOPE_DIM = 128
B_ROPE_DIM = 64
B_V_DIM = 128
B_Q_LORA = 512
B_KV_LORA = 256
B_WIDTH = B_HEADS * B_V_DIM

AB_IN = 3 * A_WIDTH + B_Q_LORA + B_KV_LORA + B_ROPE_DIM
AB_OUT = A_WIDTH + B_WIDTH

C_HEADS = 8
C_HEAD_DIM = 128
C_V_DIM = 2 * C_HEAD_DIM
C_ROT_DIM = C_HEAD_DIM // 4
C_IN = C_HEADS * (4 * C_HEAD_DIM + C_V_DIM)
C_OUT = C_HEADS * C_V_DIM

N_EVEN = (DEPTH + 1) // 2
N_ODD = DEPTH // 2

kernel_name = 'hybrid_streaming_encoder_block'


def rmsnorm(x, g):
    xf = x.astype(jnp.float32)
    y = xf * lax.rsqrt(jnp.mean(xf * xf, axis=-1, keepdims=True) + NORM_EPS)
    return (y * g.astype(jnp.float32)).astype(x.dtype)


def swiglu(h, w_in, w_out):
    a, b = jnp.split(h @ w_in, 2, axis=-1)
    return (jax.nn.silu(a) * b) @ w_out


def rope_tables(seq, rot_dim, dtype):
    inv = ROPE_THETA ** (-jnp.arange(0, rot_dim, 2, dtype=jnp.float32) / rot_dim)
    ang = jnp.arange(seq, dtype=jnp.float32)[:, None] * inv[None, :]
    return jnp.cos(ang).astype(dtype), jnp.sin(ang).astype(dtype)


def apply_rope(x, cos, sin):
    c = cos[None, :, None, :]
    s = sin[None, :, None, :]
    x1, x2 = jnp.split(x, 2, axis=-1)
    return jnp.concatenate([x1 * c - x2 * s, x2 * c + x1 * s], axis=-1)


def partial_rope(x, cos, sin, rot):
    return jnp.concatenate([apply_rope(x[..., :rot], cos, sin), x[..., rot:]], axis=-1)


def chunk_mask(qs, qe, ke):
    qc = jnp.arange(qs, qe) // CHUNK
    kc = jnp.arange(ke) // CHUNK
    return kc[None, :] <= qc[:, None]


def chunked_relpos_attention(q, k, v, rel_table):
    b, s, h, d = q.shape
    nc = s // CHUNK
    band = BAND_CHUNKS * CHUNK
    qc = q.reshape(b, nc, CHUNK, h, d)
    pad = ((0, 0), (N_PREV_CHUNKS, 0), (0, 0), (0, 0), (0, 0))
    kp = jnp.pad(k.reshape(b, nc, CHUNK, h, d), pad)
    vp = jnp.pad(v.reshape(b, nc, CHUNK, h, d), pad)
    idx = jnp.arange(nc)[:, None] + jnp.arange(BAND_CHUNKS)[None, :]
    kb = kp[:, idx].reshape(b, nc, band, h, d)
    vb = vp[:, idx].reshape(b, nc, band, h, d)
    scores = jnp.einsum('bnqhd,bnkhd->bhnqk', qc, kb).astype(jnp.float32) * (d ** -0.5)
    dist = jnp.arange(CHUNK)[:, None] + N_PREV_CHUNKS * CHUNK - jnp.arange(band)[None, :]
    bias = rel_table[:, jnp.clip(dist, -REL_FUTURE, REL_PAST_CLIP) + REL_FUTURE]
    scores = scores + bias[None, :, None].astype(jnp.float32)
    valid = (jnp.arange(nc)[:, None] - N_PREV_CHUNKS + jnp.arange(band)[None, :] // CHUNK) >= 0
    scores = jnp.where(valid[None, None, :, None, :], scores, NEG_INF)
    probs = jax.nn.softmax(scores, axis=-1).astype(v.dtype)
    out = jnp.einsum('bhnqk,bnkhd->bnqhd', probs, vb)
    return out.reshape(b, s, h * d)


def mla_attention(cq, ckv, kr, g_q, w_qup, g_kv, w_kvup, cos, sin):
    b, s, _ = cq.shape
    q = (rmsnorm(cq, g_q) @ w_qup).reshape(b, s, B_HEADS, B_NOPE_DIM + B_ROPE_DIM)
    q_nope = q[..., :B_NOPE_DIM]
    q_rope = apply_rope(q[..., B_NOPE_DIM:], cos, sin)
    kv = (rmsnorm(ckv, g_kv) @ w_kvup).reshape(b, s, B_HEADS, B_NOPE_DIM + B_V_DIM)
    k_nope = kv[..., :B_NOPE_DIM]
    v = kv[..., B_NOPE_DIM:]
    k_rope = apply_rope(kr[:, :, None, :], cos, sin)[:, :, 0, :]
    scale = (B_NOPE_DIM + B_ROPE_DIM) ** -0.5
    outs = []
    for qs in range(0, s, Q_BLOCK):
        qe = qs + Q_BLOCK
        sc = (jnp.einsum('bqhd,bkhd->bhqk', q_nope[:, qs:qe], k_nope[:, :qe])
              + jnp.einsum('bqhr,bkr->bhqk', q_rope[:, qs:qe], k_rope[:, :qe])).astype(jnp.float32) * scale
        sc = jnp.where(chunk_mask(qs, qe, qe)[None, None], sc, NEG_INF)
        pr = jax.nn.softmax(sc, axis=-1).astype(v.dtype)
        outs.append(jnp.einsum('bhqk,bkhd->bqhd', pr, v[:, :qe]))
    return jnp.concatenate(outs, axis=1).reshape(b, s, B_WIDTH)


def diff_attention(u, lq1, lk1, lq2, lk2, g_sub, lambda_init, cos, sin):
    b, s, _ = u.shape
    u = u.reshape(b, s, C_HEADS, 4 * C_HEAD_DIM + C_V_DIM)
    q1, q2, k1, k2 = [partial_rope(u[..., i * C_HEAD_DIM:(i + 1) * C_HEAD_DIM], cos, sin, C_ROT_DIM)
                      for i in range(4)]
    v = u[..., 4 * C_HEAD_DIM:]
    lam = (jnp.exp(jnp.sum(lq1.astype(jnp.float32) * lk1.astype(jnp.float32)))
           - jnp.exp(jnp.sum(lq2.astype(jnp.float32) * lk2.astype(jnp.float32))) + lambda_init)
    scale = C_HEAD_DIM ** -0.5
    outs = []
    for qs in range(0, s, Q_BLOCK):
        qe = qs + Q_BLOCK
        mask = chunk_mask(qs, qe, qe)[None, None]
        s1 = jnp.einsum('bqhd,bkhd->bhqk', q1[:, qs:qe], k1[:, :qe]).astype(jnp.float32) * scale
        s2 = jnp.einsum('bqhd,bkhd->bhqk', q2[:, qs:qe], k2[:, :qe]).astype(jnp.float32) * scale
        w = (jax.nn.softmax(jnp.where(mask, s1, NEG_INF), axis=-1)
             - lam * jax.nn.softmax(jnp.where(mask, s2, NEG_INF), axis=-1))
        outs.append(jnp.einsum('bhqk,bkhd->bqhd', w.astype(v.dtype), v[:, :qe]))
    o = jnp.concatenate(outs, axis=1)
    o = rmsnorm(o, g_sub) * (1.0 - lambda_init)
    return o.reshape(b, s, C_OUT)


def setup_inputs(seed: int = 0) -> dict:
    key = jax.random.key(seed)
    ks = iter(jax.random.split(key, 40))
    f32 = jnp.float32

    def w(shape, fan_in):
        return jax.random.normal(next(ks), shape, f32) * (fan_in ** -0.5)

    def gain(shape):
        return 1.0 + 0.05 * jax.random.normal(next(ks), shape, f32)

    def small(shape, sd):
        return sd * jax.random.normal(next(ks), shape, f32)

    return {
        'x': jax.random.normal(next(ks), (BATCH, SEQ, D_MODEL), f32),
        'p': jax.random.normal(next(ks), (DEPTH, BATCH, SEQ, PLE_DIM), f32),
        'ffn1_g_pre': gain((DEPTH, D_MODEL)),
        'ffn1_w_in': w((DEPTH, D_MODEL, 2 * D_FF), D_MODEL),
        'ffn1_w_out': w((DEPTH, D_FF, D_MODEL), D_FF),
        'ffn1_g_post': gain((DEPTH, D_MODEL)),
        'mix_g_pre': gain((DEPTH, D_MODEL)),
        'mix_g_post': gain((DEPTH, D_MODEL)),
        'ab_w_in': w((N_EVEN, D_MODEL, AB_IN), D_MODEL),
        'a_rel_bias': small((N_EVEN, A_HEADS, REL_TABLE), 0.2),
        'b_g_q': gain((N_EVEN, B_Q_LORA)),
        'b_w_qup': w((N_EVEN, B_Q_LORA, B_HEADS * (B_NOPE_DIM + B_ROPE_DIM)), B_Q_LORA),
        'b_g_kv': gain((N_EVEN, B_KV_LORA)),
        'b_w_kvup': w((N_EVEN, B_KV_LORA, B_HEADS * (B_NOPE_DIM + B_V_DIM)), B_KV_LORA),
        'ab_w_out': w((N_EVEN, AB_OUT, D_MODEL), AB_OUT),
        'c_w_in': w((N_ODD, D_MODEL, C_IN), D_MODEL),
        'c_lq1': small((N_ODD, C_HEAD_DIM), 0.1),
        'c_lk1': small((N_ODD, C_HEAD_DIM), 0.1),
        'c_lq2': small((N_ODD, C_HEAD_DIM), 0.1),
        'c_lk2': small((N_ODD, C_HEAD_DIM), 0.1),
        'c_g_sub': gain((N_ODD, C_V_DIM)),
        'c_w_out': w((N_ODD, C_OUT, D_MODEL), C_OUT),
        'ffn2_g_pre': gain((DEPTH, D_MODEL)),
        'ffn2_w_in': w((DEPTH, D_MODEL, 2 * D_FF), D_MODEL),
        'ffn2_w_out': w((DEPTH, D_FF, D_MODEL), D_FF),
        'ffn2_g_post': gain((DEPTH, D_MODEL)),
        'ple_g_pre': gain((DEPTH, D_MODEL)),
        'ple_w_gate': w((DEPTH, D_MODEL, D_MODEL), D_MODEL),
        'ple_w_proj': w((DEPTH, PLE_DIM, D_MODEL), PLE_DIM),
        'ple_g_post': gain((DEPTH, D_MODEL)),
    }


def reference(x, p, ffn1_g_pre, ffn1_w_in, ffn1_w_out, ffn1_g_post, mix_g_pre, mix_g_post,
              ab_w_in, a_rel_bias, b_g_q, b_w_qup, b_g_kv, b_w_kvup, ab_w_out,
              c_w_in, c_lq1, c_lk1, c_lq2, c_lk2, c_g_sub, c_w_out,
              ffn2_g_pre, ffn2_w_in, ffn2_w_out, ffn2_g_post,
              ple_g_pre, ple_w_gate, ple_w_proj, ple_g_post):
    b, s, _ = x.shape
    cos_b, sin_b = rope_tables(s, B_ROPE_DIM, x.dtype)
    cos_c, sin_c = rope_tables(s, C_ROT_DIM, x.dtype)
    split_at = [A_WIDTH, 2 * A_WIDTH, 3 * A_WIDTH, 3 * A_WIDTH + B_Q_LORA,
                3 * A_WIDTH + B_Q_LORA + B_KV_LORA]
    h = x
    for i in range(DEPTH):
        j = i // 2
        h = h + 0.5 * rmsnorm(swiglu(rmsnorm(h, ffn1_g_pre[i]), ffn1_w_in[i], ffn1_w_out[i]), ffn1_g_post[i])
        hn = rmsnorm(h, mix_g_pre[i])
        if i % 2 == 0:
            u = hn @ ab_w_in[j]
            qa, ka, va, cq, ckv, kr = jnp.split(u, split_at, axis=-1)
            oa = chunked_relpos_attention(qa.reshape(b, s, A_HEADS, A_HEAD_DIM),
                                          ka.reshape(b, s, A_HEADS, A_HEAD_DIM),
                                          va.reshape(b, s, A_HEADS, A_HEAD_DIM), a_rel_bias[j])
            ob = mla_attention(cq, ckv, kr, b_g_q[j], b_w_qup[j], b_g_kv[j], b_w_kvup[j], cos_b, sin_b)
            mix = jnp.concatenate([oa, ob], axis=-1) @ ab_w_out[j]
        else:
            lambda_init = 0.8 - 0.6 * math.exp(-0.3 * i)
            oc = diff_attention(hn @ c_w_in[j], c_lq1[j], c_lk1[j], c_lq2[j], c_lk2[j], c_g_sub[j],
                                lambda_init, cos_c, sin_c)
            mix = oc @ c_w_out[j]
        h = h + rmsnorm(mix, mix_g_post[i])
        h = h + 0.5 * rmsnorm(swiglu(rmsnorm(h, ffn2_g_pre[i]), ffn2_w_in[i], ffn2_w_out[i]), ffn2_g_post[i])
        gate = jax.nn.sigmoid(rmsnorm(h, ple_g_pre[i]) @ ple_w_gate[i])
        h = h + rmsnorm(gate * (p[i] @ ple_w_proj[i]), ple_g_post[i])
    return h
```

```python
import functools
import math

import jax
import jax.numpy as jnp
from jax import lax
from jax.experimental import pallas as pl
from jax.experimental.pallas import tpu as pltpu

D_MODEL = 2048
DEPTH = 4
CHUNK = 64
N_PREV_CHUNKS = 8
ROPE_THETA = 500000.0
PLE_DIM = 256
D_FF = 5632
NORM_EPS = 1e-6
NEG_INF = -1e30

HEADS = 8
HEAD_DIM = 128
A_WIDTH = HEADS * HEAD_DIM
REL_FUTURE = CHUNK - 1
REL_PAST_CLIP = 128
REL_TABLE = REL_FUTURE + REL_PAST_CLIP + 1

B_NOPE_DIM = 128
B_ROPE_DIM = 64
B_Q_LORA = 512
B_KV_LORA = 256
B_IN = B_Q_LORA + B_KV_LORA + 128

C_V_DIM = 2 * HEAD_DIM
C_ROT_DIM = HEAD_DIM // 4

LANES = 128
VMEM_LIMIT = 56 * 1024 * 1024

TM_FFN = 512
TF_FFN = 512
TM_PROJ = 512
TN_PROJ = 512
TQ = 256
A_WIN = N_PREV_CHUNKS * CHUNK + TQ

BF16 = jnp.bfloat16
F32 = jnp.float32


def _rms(x, g):
    return x * lax.rsqrt(jnp.mean(x * x, axis=-1, keepdims=True) + NORM_EPS) * g


def _dot(a, b):
    return jnp.dot(a, b, preferred_element_type=F32)


def _dot_t(a, b):
    return lax.dot_general(a, b, (((1,), (1,)), ((), ())), preferred_element_type=F32)


def _rope_group(u, cos, sin_up, sin_dn, half):
    return (u * cos + pltpu.roll(u, half, 1) * sin_up
            + pltpu.roll(u, LANES - half, 1) * sin_dn)


def _params(*sem):
    return pltpu.CompilerParams(dimension_semantics=sem, vmem_limit_bytes=VMEM_LIMIT)


def _ffn_kernel(h_ref, gpre_ref, wa_ref, wb_ref, wo_ref, gpost_ref, o_ref, hn_ref, acc_ref):
    j = pl.program_id(1)

    @pl.when(j == 0)
    def _():
        hn_ref[...] = _rms(h_ref[...], gpre_ref[...]).astype(BF16)

    hn = hn_ref[...]
    a = _dot(hn, wa_ref[...])
    b = _dot(hn, wb_ref[...])
    g = (a * jax.nn.sigmoid(a) * b).astype(BF16)
    part = _dot(g, wo_ref[...])

    @pl.when(j == 0)
    def _():
        acc_ref[...] = part

    @pl.when(j > 0)
    def _():
        acc_ref[...] += part

    @pl.when(j == pl.num_programs(1) - 1)
    def _():
        o_ref[...] = h_ref[...] + 0.5 * _rms(acc_ref[...], gpost_ref[...])


def _ffn(h, g_pre, w_in, w_out, g_post):
    t = h.shape[0]
    nf = D_FF // TF_FFN
    return pl.pallas_call(
        _ffn_kernel,
        name="ffn",
        grid=(t // TM_FFN, nf),
        in_specs=[
            pl.BlockSpec((TM_FFN, D_MODEL), lambda i, j: (i, 0)),
            pl.BlockSpec((1, D_MODEL), lambda i, j: (0, 0)),
            pl.BlockSpec((D_MODEL, TF_FFN), lambda i, j: (0, j)),
            pl.BlockSpec((D_MODEL, TF_FFN), lambda i, j: (0, j + nf)),
            pl.BlockSpec((TF_FFN, D_MODEL), lambda i, j: (j, 0)),
            pl.BlockSpec((1, D_MODEL), lambda i, j: (0, 0)),
        ],
        out_specs=pl.BlockSpec((TM_FFN, D_MODEL), lambda i, j: (i, 0)),
        out_shape=jax.ShapeDtypeStruct((t, D_MODEL), F32),
        scratch_shapes=[pltpu.VMEM((TM_FFN, D_MODEL), BF16),
                        pltpu.VMEM((TM_FFN, D_MODEL), F32)],
        compiler_params=_params("parallel", "arbitrary"),
    )(h, g_pre, w_in, w_in, w_out, g_post)


def _proj_kernel(h_ref, g_ref, w_ref, cos_ref, sup_ref, sdn_ref, o_ref, hn_ref, *, rope_tiles, half):
    j = pl.program_id(1)

    @pl.when(j == 0)
    def _():
        hn_ref[...] = _rms(h_ref[...], g_ref[...]).astype(BF16)

    u = _dot(hn_ref[...], w_ref[...])

    if rope_tiles:
        @pl.when(j < rope_tiles)
        def _():
            cos, sup, sdn = cos_ref[...], sup_ref[...], sdn_ref[...]
            for c in range(TN_PROJ // LANES):
                sl = slice(c * LANES, (c + 1) * LANES)
                o_ref[:, sl] = _rope_group(u[:, sl], cos, sup, sdn, half).astype(o_ref.dtype)

        @pl.when(j >= rope_tiles)
        def _():
            o_ref[...] = u.astype(o_ref.dtype)
    else:
        o_ref[...] = u.astype(o_ref.dtype)


def _proj(h, g, w, tables, seq, rope_cols, half):
    t, n = h.shape[0], w.shape[1]
    blocks_per_seq = seq // TM_PROJ
    tab_spec = pl.BlockSpec((TM_PROJ, LANES), lambda i, j: (i % blocks_per_seq, 0))
    kern = functools.partial(_proj_kernel, rope_tiles=rope_cols // TN_PROJ, half=half)
    return pl.pallas_call(
        kern,
        name="in_proj",
        grid=(t // TM_PROJ, n // TN_PROJ),
        in_specs=[
            pl.BlockSpec((TM_PROJ, D_MODEL), lambda i, j: (i, 0)),
            pl.BlockSpec((1, D_MODEL), lambda i, j: (0, 0)),
            pl.BlockSpec((D_MODEL, TN_PROJ), lambda i, j: (0, j)),
            tab_spec, tab_spec, tab_spec,
        ],
        out_specs=pl.BlockSpec((TM_PROJ, TN_PROJ), lambda i, j: (i, j)),
        out_shape=jax.ShapeDtypeStruct((t, n), BF16),
        scratch_shapes=[pltpu.VMEM((TM_PROJ, D_MODEL), BF16)],
        compiler_params=_params("parallel", "arbitrary"),
    )(h, g, w, *tables)


def _mla_proj_kernel(h_ref, g_ref, wb_ref, gq_ref, wq_ref, gkv_ref, wkv_ref,
                     cos_ref, sup_ref, sdn_ref, q_ref, k_ref, v_ref):
    hn = _rms(h_ref[...], g_ref[...]).astype(BF16)
    u = _dot(hn, wb_ref[...])
    cq = u[:, :B_Q_LORA]
    ckv = u[:, B_Q_LORA:B_Q_LORA + B_KV_LORA]
    kr = u[:, B_Q_LORA + B_KV_LORA:]
    cos, sup, sdn = cos_ref[...], sup_ref[...], sdn_ref[...]
    half = B_ROPE_DIM // 2

    q = _dot(_rms(cq, gq_ref[...]).astype(BF16), wq_ref[...])
    kv = _dot(_rms(ckv, gkv_ref[...]).astype(BF16), wkv_ref[...])
    k_rope = _rope_group(kr, cos, sup, sdn, half).astype(BF16)
    for hd in range(HEADS):
        lo, mid, hi = 2 * hd * LANES, (2 * hd + 1) * LANES, (2 * hd + 2) * LANES
        q_ref[:, lo:mid] = q[:, lo:mid].astype(BF16)
        q_ref[:, mid:hi] = _rope_group(q[:, mid:hi], cos, sup, sdn, half).astype(BF16)
        k_ref[:, lo:mid] = kv[:, lo:mid].astype(BF16)
        k_ref[:, mid:hi] = k_rope
        v_ref[:, hd * LANES:(hd + 1) * LANES] = kv[:, mid:hi].astype(BF16)


def _mla_proj(h, g, wb, gq, wq, gkv, wkv, tables, seq):
    t = h.shape[0]
    blocks_per_seq = seq // TM_PROJ
    tab_spec = pl.BlockSpec((TM_PROJ, LANES), lambda i: (i % blocks_per_seq, 0))
    full = lambda a: pl.BlockSpec(a.shape, lambda i: (0, 0))
    wide = HEADS * 2 * LANES
    return pl.pallas_call(
        _mla_proj_kernel,
        name="mla_proj",
        grid=(t // TM_PROJ,),
        in_specs=[pl.BlockSpec((TM_PROJ, D_MODEL), lambda i: (i, 0)),
                  full(g), full(wb), full(gq), full(wq), full(gkv), full(wkv),
                  tab_spec, tab_spec, tab_spec],
        out_specs=[pl.BlockSpec((TM_PROJ, wide), lambda i: (i, 0)),
                   pl.BlockSpec((TM_PROJ, wide), lambda i: (i, 0)),
                   pl.BlockSpec((TM_PROJ, A_WIDTH), lambda i: (i, 0))],
        out_shape=[jax.ShapeDtypeStruct((t, wide), BF16),
                   jax.ShapeDtypeStruct((t, wide), BF16),
                   jax.ShapeDtypeStruct((t, A_WIDTH), BF16)],
        compiler_params=_params("parallel"),
    )(h, g, wb, gq, wq, gkv, wkv, *tables)


def _chunk_of(pos):
    return jnp.right_shift(pos, CHUNK.bit_length() - 1)


def _chunk_causal_mask():
    qc = _chunk_of(lax.broadcasted_iota(jnp.int32, (TQ, TQ), 0))
    kc = _chunk_of(lax.broadcasted_iota(jnp.int32, (TQ, TQ), 1))
    return kc <= qc


def _causal_scores(q, k_ref, qs, scale, mask):
    s_diag = jnp.where(mask, _dot_t(q, k_ref[qs:qs + TQ, :]) * scale, NEG_INF)
    m = jnp.max(s_diag, axis=-1, keepdims=True)
    if qs == 0:
        return None, s_diag, m
    s_off = _dot_t(q, k_ref[0:qs, :]) * scale
    return s_off, s_diag, jnp.maximum(m, jnp.max(s_off, axis=-1, keepdims=True))


def _mla_attn_kernel(q_ref, k_ref, v_ref, o_ref, *, seq, scale):
    mask = _chunk_causal_mask()
    for qs in range(0, seq, TQ):
        s_off, s_diag, m = _causal_scores(q_ref[qs:qs + TQ, :], k_ref, qs, scale, mask)
        p = jnp.exp(s_diag - m)
        l = jnp.sum(p, axis=-1, keepdims=True)
        o = _dot(p.astype(BF16), v_ref[qs:qs + TQ, :])
        if s_off is not None:
            p = jnp.exp(s_off - m)
            l = l + jnp.sum(p, axis=-1, keepdims=True)
            o = o + _dot(p.astype(BF16), v_ref[0:qs, :])
        o_ref[qs:qs + TQ, :] = (o / l).astype(o_ref.dtype)


def _mla_attn(q, k, v, batch, seq):
    t = q.shape[0]
    scale = (B_NOPE_DIM + B_ROPE_DIM) ** -0.5
    return pl.pallas_call(
        functools.partial(_mla_attn_kernel, seq=seq, scale=scale),
        name="mla_attn",
        grid=(batch, HEADS),
        in_specs=[pl.BlockSpec((seq, 2 * LANES), lambda b, h: (b, h)),
                  pl.BlockSpec((seq, 2 * LANES), lambda b, h: (b, h)),
                  pl.BlockSpec((seq, LANES), lambda b, h: (b, h))],
        out_specs=pl.BlockSpec((seq, LANES), lambda b, h: (b, h)),
        out_shape=jax.ShapeDtypeStruct((t, A_WIDTH), BF16),
        compiler_params=_params("parallel", "parallel"),
    )(q, k, v)


def _diff_attn_kernel(q1_ref, q2_ref, k1_ref, k2_ref, v_ref, lq1_ref, lk1_ref, lq2_ref, lk2_ref,
                      gsub_ref, o_ref, *, seq, scale, lambda_init):
    mask = _chunk_causal_mask()
    lam = (jnp.exp(jnp.sum(lq1_ref[...] * lk1_ref[...], axis=-1, keepdims=True))
           - jnp.exp(jnp.sum(lq2_ref[...] * lk2_ref[...], axis=-1, keepdims=True)) + lambda_init)
    for qs in range(0, seq, TQ):
        off1, diag1, m1 = _causal_scores(q1_ref[qs:qs + TQ, :], k1_ref, qs, scale, mask)
        off2, diag2, m2 = _causal_scores(q2_ref[qs:qs + TQ, :], k2_ref, qs, scale, mask)
        p1d, p2d = jnp.exp(diag1 - m1), jnp.exp(diag2 - m2)
        l1 = jnp.sum(p1d, axis=-1, keepdims=True)
        l2 = jnp.sum(p2d, axis=-1, keepdims=True)
        if off1 is not None:
            p1o, p2o = jnp.exp(off1 - m1), jnp.exp(off2 - m2)
            l1 = l1 + jnp.sum(p1o, axis=-1, keepdims=True)
            l2 = l2 + jnp.sum(p2o, axis=-1, keepdims=True)
        r1 = 1.0 / l1
        r2 = lam / l2
        o = _dot((p1d * r1 - p2d * r2).astype(BF16), v_ref[qs:qs + TQ, :])
        if off1 is not None:
            o = o + _dot((p1o * r1 - p2o * r2).astype(BF16), v_ref[0:qs, :])
        o_ref[qs:qs + TQ, :] = (_rms(o, gsub_ref[...]) * (1.0 - lambda_init)).astype(o_ref.dtype)


def _diff_attn(u, lq1, lk1, lq2, lk2, g_sub, lambda_init, batch, seq):
    t = u.shape[0]
    part = lambda c: pl.BlockSpec((seq, LANES), lambda b, h: (b, c * HEADS + h))
    vec = pl.BlockSpec((1, LANES), lambda b, h: (0, 0))
    return pl.pallas_call(
        functools.partial(_diff_attn_kernel, seq=seq, scale=HEAD_DIM ** -0.5, lambda_init=lambda_init),
        name="diff_attn",
        grid=(batch, HEADS),
        in_specs=[part(0), part(1), part(2), part(3),
                  pl.BlockSpec((seq, C_V_DIM), lambda b, h: (b, 2 * HEADS + h)),
                  vec, vec, vec, vec,
                  pl.BlockSpec((1, C_V_DIM), lambda b, h: (0, 0))],
        out_specs=pl.BlockSpec((seq, C_V_DIM), lambda b, h: (b, h)),
        out_shape=jax.ShapeDtypeStruct((t, HEADS * C_V_DIM), BF16),
        compiler_params=_params("parallel", "parallel"),
    )(u, u, u, u, u, lq1, lk1, lq2, lk2, g_sub)


def _band_bias_kernel(tab_ref, o_ref):
    hd = pl.program_id(0)
    qi = lax.broadcasted_iota(jnp.int32, (TQ, A_WIN), 0)
    kj = lax.broadcasted_iota(jnp.int32, (TQ, A_WIN), 1)
    idx = jnp.clip(qi - kj + N_PREV_CHUNKS * CHUNK, -REL_FUTURE, REL_PAST_CLIP) + REL_FUTURE
    qc, kc = _chunk_of(qi), _chunk_of(kj)
    valid = (kc >= qc) & (kc <= qc + N_PREV_CHUNKS)

    def body(tt, acc):
        return jnp.where(idx == tt, tab_ref[hd, tt], acc)

    bias = lax.fori_loop(0, REL_TABLE, body, jnp.zeros((TQ, A_WIN), F32))
    o_ref[0] = jnp.where(valid, bias, NEG_INF)


def _band_bias(rel_table):
    return pl.pallas_call(
        _band_bias_kernel,
        name="band_bias",
        grid=(HEADS,),
        in_specs=[pl.BlockSpec(memory_space=pltpu.SMEM)],
        out_specs=pl.BlockSpec((1, TQ, A_WIN), lambda h: (h, 0, 0)),
        out_shape=jax.ShapeDtypeStruct((HEADS, TQ, A_WIN), F32),
        compiler_params=_params("parallel"),
    )(rel_table)


def _band_attn_kernel(q_ref, k_ref, v_ref, bias_ref, o_ref, *, seq, scale):
    for qs in range(0, seq, TQ):
        ws = max(0, qs - N_PREV_CHUNKS * CHUNK)
        s = _dot_t(q_ref[qs:qs + TQ, :], k_ref[ws:qs + TQ, :]) * scale
        s = s + bias_ref[0, :, A_WIN - (qs + TQ - ws):]
        p = jnp.exp(s - jnp.max(s, axis=-1, keepdims=True))
        l = jnp.sum(p, axis=-1, keepdims=True)
        o = _dot(p.astype(BF16), v_ref[ws:qs + TQ, :])
        o_ref[qs:qs + TQ, :] = (o / l).astype(o_ref.dtype)


def _band_attn(u, bias, batch, seq):
    t = u.shape[0]
    part = lambda c: pl.BlockSpec((seq, LANES), lambda b, h: (b, c * HEADS + h))
    return pl.pallas_call(
        functools.partial(_band_attn_kernel, seq=seq, scale=HEAD_DIM ** -0.5),
        name="band_attn",
        grid=(batch, HEADS),
        in_specs=[part(0), part(1), part(2),
                  pl.BlockSpec((1, TQ, A_WIN), lambda b, h: (h, 0, 0))],
        out_specs=pl.BlockSpec((seq, LANES), lambda b, h: (b, h)),
        out_shape=jax.ShapeDtypeStruct((t, A_WIDTH), BF16),
        compiler_params=_params("parallel", "parallel"),
    )(u, u, u, bias)


def _out_proj_kernel(*refs, n_in):
    x_refs, (w_ref, g_ref, h_ref, o_ref) = refs[:n_in], refs[n_in:]
    mix = None
    k0 = 0
    for x_ref in x_refs:
        k1 = k0 + x_ref.shape[1]
        part = _dot(x_ref[...], w_ref[k0:k1, :])
        mix = part if mix is None else mix + part
        k0 = k1
    o_ref[...] = h_ref[...] + _rms(mix, g_ref[...])


def _out_proj(xs, w, g, h):
    t = h.shape[0]
    return pl.pallas_call(
        functools.partial(_out_proj_kernel, n_in=len(xs)),
        name="out_proj",
        grid=(t // TM_PROJ,),
        in_specs=[pl.BlockSpec((TM_PROJ, x.shape[1]), lambda i: (i, 0)) for x in xs] + [
            pl.BlockSpec(w.shape, lambda i: (0, 0), pipeline_mode=pl.Buffered(1)),
            pl.BlockSpec((1, D_MODEL), lambda i: (0, 0)),
            pl.BlockSpec((TM_PROJ, D_MODEL), lambda i: (i, 0)),
        ],
        out_specs=pl.BlockSpec((TM_PROJ, D_MODEL), lambda i: (i, 0)),
        out_shape=jax.ShapeDtypeStruct((t, D_MODEL), F32),
        compiler_params=_params("parallel"),
    )(*xs, w, g, h)


def _ple_kernel(h_ref, gpre_ref, wg_ref, p_ref, wp_ref, gpost_ref, o_ref):
    h = h_ref[...]
    gate = jax.nn.sigmoid(_dot(_rms(h, gpre_ref[...]).astype(BF16), wg_ref[...]))
    emb = _dot(p_ref[...].astype(BF16), wp_ref[...])
    o_ref[...] = h + _rms(gate * emb, gpost_ref[...])


def _ple(h, g_pre, w_gate, p, w_proj, g_post):
    t = h.shape[0]
    return pl.pallas_call(
        _ple_kernel,
        name="gated_embed",
        grid=(t // TM_PROJ,),
        in_specs=[
            pl.BlockSpec((TM_PROJ, D_MODEL), lambda i: (i, 0)),
            pl.BlockSpec((1, D_MODEL), lambda i: (0, 0)),
            pl.BlockSpec(w_gate.shape, lambda i: (0, 0), pipeline_mode=pl.Buffered(1)),
            pl.BlockSpec((TM_PROJ, PLE_DIM), lambda i: (i, 0)),
            pl.BlockSpec(w_proj.shape, lambda i: (0, 0), pipeline_mode=pl.Buffered(1)),
            pl.BlockSpec((1, D_MODEL), lambda i: (0, 0)),
        ],
        out_specs=pl.BlockSpec((TM_PROJ, D_MODEL), lambda i: (i, 0)),
        out_shape=jax.ShapeDtypeStruct((t, D_MODEL), F32),
        compiler_params=_params("parallel"),
    )(h, g_pre, w_gate, p, w_proj, g_post)


def _rope_tables(seq, rot_dim):
    inv = ROPE_THETA ** (-jnp.arange(0, rot_dim, 2, dtype=F32) / rot_dim)
    ang = jnp.arange(seq, dtype=F32)[:, None] * inv[None, :]
    return jnp.cos(ang), jnp.sin(ang)


def _lane_tables(cos, sin, tail):
    seq, half = cos.shape
    rest = LANES - 2 * half
    zeros = lambda n: jnp.zeros((seq, n), F32)
    cos_t = jnp.concatenate([cos, cos, jnp.full((seq, rest), tail, F32)], axis=1)
    sin_up = jnp.concatenate([zeros(half), sin, zeros(rest)], axis=1)
    sin_dn = jnp.concatenate([-sin, zeros(LANES - half)], axis=1)
    return cos_t, sin_up, sin_dn


def _row(v):
    return v.reshape(1, -1)


def kernel(x, p, ffn1_g_pre, ffn1_w_in, ffn1_w_out, ffn1_g_post, mix_g_pre, mix_g_post, ab_w_in, a_rel_bias, b_g_q, b_w_qup, b_g_kv, b_w_kvup, ab_w_out, c_w_in, c_lq1, c_lk1, c_lq2, c_lk2, c_g_sub, c_w_out, ffn2_g_pre, ffn2_w_in, ffn2_w_out, ffn2_g_post, ple_g_pre, ple_w_gate, ple_w_proj, ple_g_post):
    batch, seq, _ = x.shape
    t = batch * seq
    assert seq % TQ == 0 and seq % TM_PROJ == 0 and t % TM_FFN == 0

    tab_b = _lane_tables(*_rope_tables(seq, B_ROPE_DIM), tail=0.0)
    tab_c = _lane_tables(*_rope_tables(seq, C_ROT_DIM), tail=1.0)

    h = x.reshape(t, D_MODEL)
    p = p.reshape(DEPTH, t, PLE_DIM)
    for i in range(DEPTH):
        j = i // 2
        h = _ffn(h, _row(ffn1_g_pre[i]), ffn1_w_in[i].astype(BF16), ffn1_w_out[i].astype(BF16),
                 _row(ffn1_g_post[i]))
        g_mix = _row(mix_g_pre[i])
        if i % 2 == 0:
            w_in = ab_w_in[j].astype(BF16)
            u_a = _proj(h, g_mix, w_in[:, :3 * A_WIDTH], tab_b, seq, rope_cols=0, half=0)
            w_b = jnp.pad(w_in[:, 3 * A_WIDTH:], ((0, 0), (0, LANES - B_ROPE_DIM)))
            w_q = b_w_qup[j].astype(BF16).reshape(B_Q_LORA, HEADS, B_NOPE_DIM + B_ROPE_DIM)
            w_q = jnp.pad(w_q, ((0, 0), (0, 0), (0, LANES - B_ROPE_DIM))).reshape(B_Q_LORA, -1)
            q_b, k_b, v_b = _mla_proj(h, g_mix, w_b, _row(b_g_q[j]), w_q, _row(b_g_kv[j]),
                                      b_w_kvup[j].astype(BF16), tab_b, seq)
            o_a = _band_attn(u_a, _band_bias(a_rel_bias[j]), batch, seq)
            o_b = _mla_attn(q_b, k_b, v_b, batch, seq)
            h = _out_proj([o_a, o_b], ab_w_out[j].astype(BF16), _row(mix_g_post[i]), h)
        else:
            lambda_init = 0.8 - 0.6 * math.exp(-0.3 * i)
            w_c = c_w_in[j].astype(BF16).reshape(D_MODEL, HEADS, 4 * HEAD_DIM + C_V_DIM)
            w_qk = w_c[:, :, :4 * HEAD_DIM].reshape(D_MODEL, HEADS, 4, HEAD_DIM)
            w_qk = w_qk.transpose(0, 2, 1, 3).reshape(D_MODEL, 4 * A_WIDTH)
            w_c = jnp.concatenate([w_qk, w_c[:, :, 4 * HEAD_DIM:].reshape(D_MODEL, -1)], axis=1)
            u_c = _proj(h, g_mix, w_c, tab_c, seq, rope_cols=4 * A_WIDTH, half=C_ROT_DIM // 2)
            o_c = _diff_attn(u_c, _row(c_lq1[j]), _row(c_lk1[j]), _row(c_lq2[j]), _row(c_lk2[j]),
                             _row(c_g_sub[j]), lambda_init, batch, seq)
            h = _out_proj([o_c], c_w_out[j].astype(BF16), _row(mix_g_post[i]), h)
        h = _ffn(h, _row(ffn2_g_pre[i]), ffn2_w_in[i].astype(BF16), ffn2_w_out[i].astype(BF16),
                 _row(ffn2_g_post[i]))
        h = _ple(h, _row(ple_g_pre[i]), ple_w_gate[i].astype(BF16), p[i],
                 ple_w_proj[i].astype(BF16), _row(ple_g_post[i]))
    return h.reshape(batch, seq, D_MODEL)
```

```python
import functools
import math

import jax
import jax.numpy as jnp
from jax import lax
from jax.experimental import pallas as pl
from jax.experimental.pallas import tpu as pltpu

D_MODEL = 2048
DEPTH = 4
CHUNK = 64
N_PREV_CHUNKS = 8
ROPE_THETA = 500000.0
PLE_DIM = 256
D_FF = 5632
NORM_EPS = 1e-6
NEG_INF = -1e30

HEADS = 8
HEAD_DIM = 128
A_WIDTH = HEADS * HEAD_DIM
REL_FUTURE = CHUNK - 1
REL_PAST_CLIP = 128
REL_TABLE = REL_FUTURE + REL_PAST_CLIP + 1

B_NOPE_DIM = 128
B_ROPE_DIM = 64
B_Q_LORA = 512
B_KV_LORA = 256
B_IN = B_Q_LORA + B_KV_LORA + 128

C_V_DIM = 2 * HEAD_DIM
C_ROT_DIM = HEAD_DIM // 4

LANES = 128
VMEM_LIMIT = 58 * 1024 * 1024

TM_FFN = 1024
TF_FFN = 512
TM_IN = 1024
TN_IN = 1024
TM_PROJ = 512
TQ = 256
A_WIN = N_PREV_CHUNKS * CHUNK + TQ

BF16 = jnp.bfloat16
F32 = jnp.float32


def _rms(x, g):
    return x * lax.rsqrt(jnp.mean(x * x, axis=-1, keepdims=True) + NORM_EPS) * g


def _dot(a, b):
    return jnp.dot(a, b, preferred_element_type=F32)


def _dot_t(a, b):
    return lax.dot_general(a, b, (((1,), (1,)), ((), ())), preferred_element_type=F32)


def _rope_group(u, cos, sin_up, sin_dn, half):
    return (u * cos + pltpu.roll(u, half, 1) * sin_up
            + pltpu.roll(u, LANES - half, 1) * sin_dn)


def _params(*sem):
    return pltpu.CompilerParams(dimension_semantics=sem, vmem_limit_bytes=VMEM_LIMIT)


def _ffn_kernel(h_ref, gpre_ref, wa_ref, wb_ref, wo_ref, gpost_ref, o_ref, hn_ref):
    j = pl.program_id(1)

    @pl.when(j == 0)
    def _():
        hn_ref[...] = _rms(h_ref[...], gpre_ref[...]).astype(BF16)
        o_ref[...] = jnp.zeros_like(o_ref)

    hn = hn_ref[...]
    a = _dot(hn, wa_ref[...])
    b = _dot(hn, wb_ref[...])
    g = (a * jax.nn.sigmoid(a) * b).astype(BF16)
    o_ref[...] += _dot(g, wo_ref[...])

    @pl.when(j == pl.num_programs(1) - 1)
    def _():
        o_ref[...] = h_ref[...] + 0.5 * _rms(o_ref[...], gpost_ref[...])


def _ffn(h, g_pre, w_in, w_out, g_post, layer):
    t = h.shape[0]
    nf = D_FF // TF_FFN
    return pl.pallas_call(
        _ffn_kernel,
        name="ffn",
        grid=(t // TM_FFN, nf),
        in_specs=[
            pl.BlockSpec((TM_FFN, D_MODEL), lambda i, j: (i, 0), pipeline_mode=pl.Buffered(1)),
            pl.BlockSpec((1, D_MODEL), lambda i, j: (0, 0)),
            pl.BlockSpec((None, D_MODEL, TF_FFN), lambda i, j: (layer, 0, j)),
            pl.BlockSpec((None, D_MODEL, TF_FFN), lambda i, j: (layer, 0, j + nf)),
            pl.BlockSpec((None, TF_FFN, D_MODEL), lambda i, j: (layer, j, 0)),
            pl.BlockSpec((1, D_MODEL), lambda i, j: (0, 0)),
        ],
        out_specs=pl.BlockSpec((TM_FFN, D_MODEL), lambda i, j: (i, 0)),
        out_shape=jax.ShapeDtypeStruct((t, D_MODEL), F32),
        scratch_shapes=[pltpu.VMEM((TM_FFN, D_MODEL), BF16)],
        compiler_params=_params("parallel", "arbitrary"),
    )(h, g_pre, w_in, w_in, w_out, g_post)


def _proj_kernel(h_ref, g_ref, w_ref, cos_ref, sup_ref, sdn_ref, o_ref, hn_ref, *, rope_tiles, half):
    j = pl.program_id(1)

    @pl.when(j == 0)
    def _():
        hn_ref[...] = _rms(h_ref[...], g_ref[...]).astype(BF16)

    @pl.when(j < rope_tiles)
    def _():
        cos, sup, sdn = cos_ref[...], sup_ref[...], sdn_ref[...]
        u = _dot(hn_ref[...], w_ref[...])
        for c in range(TN_IN // LANES):
            sl = slice(c * LANES, (c + 1) * LANES)
            o_ref[:, sl] = _rope_group(u[:, sl], cos, sup, sdn, half).astype(o_ref.dtype)

    @pl.when(j >= rope_tiles)
    def _():
        o_ref[...] = _dot(hn_ref[...], w_ref[...]).astype(o_ref.dtype)


def _proj(h, g, w, tables, seq, rope_cols, half):
    t, n = h.shape[0], w.shape[1]
    blocks_per_seq = seq // TM_IN
    tab_spec = pl.BlockSpec((TM_IN, LANES), lambda i, j: (i % blocks_per_seq, 0))
    kern = functools.partial(_proj_kernel, rope_tiles=rope_cols // TN_IN, half=half)
    return pl.pallas_call(
        kern,
        name="in_proj",
        grid=(t // TM_IN, n // TN_IN),
        in_specs=[
            pl.BlockSpec((TM_IN, D_MODEL), lambda i, j: (i, 0)),
            pl.BlockSpec((1, D_MODEL), lambda i, j: (0, 0)),
            pl.BlockSpec((D_MODEL, TN_IN), lambda i, j: (0, j)),
            tab_spec, tab_spec, tab_spec,
        ],
        out_specs=pl.BlockSpec((TM_IN, TN_IN), lambda i, j: (i, j)),
        out_shape=jax.ShapeDtypeStruct((t, n), BF16),
        scratch_shapes=[pltpu.VMEM((TM_IN, D_MODEL), BF16)],
        compiler_params=_params("parallel", "arbitrary"),
    )(h, g, w, *tables)


def _mla_proj_kernel(h_ref, g_ref, wb_ref, gq_ref, wq_ref, gkv_ref, wkv_ref,
                     cos_ref, sup_ref, sdn_ref, q_ref, k_ref, v_ref):
    hn = _rms(h_ref[...], g_ref[...]).astype(BF16)
    u = _dot(hn, wb_ref[...])
    cq = u[:, :B_Q_LORA]
    ckv = u[:, B_Q_LORA:B_Q_LORA + B_KV_LORA]
    kr = u[:, B_Q_LORA + B_KV_LORA:]
    cos, sup, sdn = cos_ref[...], sup_ref[...], sdn_ref[...]
    half = B_ROPE_DIM // 2

    q = _dot(_rms(cq, gq_ref[...]).astype(BF16), wq_ref[...])
    kv = _dot(_rms(ckv, gkv_ref[...]).astype(BF16), wkv_ref[...])
    k_rope = _rope_group(kr, cos, sup, sdn, half).astype(BF16)
    for hd in range(HEADS):
        lo, mid, hi = 2 * hd * LANES, (2 * hd + 1) * LANES, (2 * hd + 2) * LANES
        q_ref[:, lo:mid] = q[:, lo:mid].astype(BF16)
        q_ref[:, mid:hi] = _rope_group(q[:, mid:hi], cos, sup, sdn, half).astype(BF16)
        k_ref[:, lo:mid] = kv[:, lo:mid].astype(BF16)
        k_ref[:, mid:hi] = k_rope
        v_ref[:, hd * LANES:(hd + 1) * LANES] = kv[:, mid:hi].astype(BF16)


def _mla_proj(h, g, wb, gq, wq, gkv, wkv, tables, seq):
    t = h.shape[0]
    blocks_per_seq = seq // TM_PROJ
    tab_spec = pl.BlockSpec((TM_PROJ, LANES), lambda i: (i % blocks_per_seq, 0))
    full = lambda a: pl.BlockSpec(a.shape, lambda i: (0, 0))
    wide = HEADS * 2 * LANES
    return pl.pallas_call(
        _mla_proj_kernel,
        name="mla_proj",
        grid=(t // TM_PROJ,),
        in_specs=[pl.BlockSpec((TM_PROJ, D_MODEL), lambda i: (i, 0)),
                  full(g), full(wb), full(gq), full(wq), full(gkv), full(wkv),
                  tab_spec, tab_spec, tab_spec],
        out_specs=[pl.BlockSpec((TM_PROJ, wide), lambda i: (i, 0)),
                   pl.BlockSpec((TM_PROJ, wide), lambda i: (i, 0)),
                   pl.BlockSpec((TM_PROJ, A_WIDTH), lambda i: (i, 0))],
        out_shape=[jax.ShapeDtypeStruct((t, wide), BF16),
                   jax.ShapeDtypeStruct((t, wide), BF16),
                   jax.ShapeDtypeStruct((t, A_WIDTH), BF16)],
        compiler_params=_params("parallel"),
    )(h, g, wb, gq, wq, gkv, wkv, *tables)


def _chunk_of(pos):
    return jnp.right_shift(pos, CHUNK.bit_length() - 1)


def _chunk_causal_mask():
    qc = _chunk_of(lax.broadcasted_iota(jnp.int32, (TQ, TQ), 0))
    kc = _chunk_of(lax.broadcasted_iota(jnp.int32, (TQ, TQ), 1))
    return kc <= qc


def _causal_scores(q, k_ref, qs, scale, mask):
    s_diag = jnp.where(mask, _dot_t(q, k_ref[qs:qs + TQ, :]) * scale, NEG_INF)
    m = jnp.max(s_diag, axis=-1, keepdims=True)
    if qs == 0:
        return None, s_diag, m
    s_off = _dot_t(q, k_ref[0:qs, :]) * scale
    return s_off, s_diag, jnp.maximum(m, jnp.max(s_off, axis=-1, keepdims=True))


def _mla_attn_kernel(q_ref, k_ref, v_ref, o_ref, *, seq, scale):
    mask = _chunk_causal_mask()
    for qs in range(0, seq, TQ):
        s_off, s_diag, m = _causal_scores(q_ref[qs:qs + TQ, :], k_ref, qs, scale, mask)
        p = jnp.exp(s_diag - m)
        l = jnp.sum(p, axis=-1, keepdims=True)
        o = _dot(p.astype(BF16), v_ref[qs:qs + TQ, :])
        if s_off is not None:
            p = jnp.exp(s_off - m)
            l = l + jnp.sum(p, axis=-1, keepdims=True)
            o = o + _dot(p.astype(BF16), v_ref[0:qs, :])
        o_ref[qs:qs + TQ, :] = (o / l).astype(o_ref.dtype)


def _mla_attn(q, k, v, batch, seq):
    t = q.shape[0]
    scale = (B_NOPE_DIM + B_ROPE_DIM) ** -0.5
    return pl.pallas_call(
        functools.partial(_mla_attn_kernel, seq=seq, scale=scale),
        name="mla_attn",
        grid=(batch, HEADS),
        in_specs=[pl.BlockSpec((seq, 2 * LANES), lambda b, h: (b, h)),
                  pl.BlockSpec((seq, 2 * LANES), lambda b, h: (b, h)),
                  pl.BlockSpec((seq, LANES), lambda b, h: (b, h))],
        out_specs=pl.BlockSpec((seq, LANES), lambda b, h: (b, h)),
        out_shape=jax.ShapeDtypeStruct((t, A_WIDTH), BF16),
        compiler_params=_params("parallel", "parallel"),
    )(q, k, v)


def _diff_attn_kernel(q1_ref, q2_ref, k1_ref, k2_ref, v_ref, lq1_ref, lk1_ref, lq2_ref, lk2_ref,
                      gsub_ref, o_ref, *, seq, scale, lambda_init):
    mask = _chunk_causal_mask()
    lam = (jnp.exp(jnp.sum(lq1_ref[...] * lk1_ref[...], axis=-1, keepdims=True))
           - jnp.exp(jnp.sum(lq2_ref[...] * lk2_ref[...], axis=-1, keepdims=True)) + lambda_init)
    for qs in range(0, seq, TQ):
        off1, diag1, m1 = _causal_scores(q1_ref[qs:qs + TQ, :], k1_ref, qs, scale, mask)
        off2, diag2, m2 = _causal_scores(q2_ref[qs:qs + TQ, :], k2_ref, qs, scale, mask)
        p1d, p2d = jnp.exp(diag1 - m1), jnp.exp(diag2 - m2)
        l1 = jnp.sum(p1d, axis=-1, keepdims=True)
        l2 = jnp.sum(p2d, axis=-1, keepdims=True)
        if off1 is not None:
            p1o, p2o = jnp.exp(off1 - m1), jnp.exp(off2 - m2)
            l1 = l1 + jnp.sum(p1o, axis=-1, keepdims=True)
            l2 = l2 + jnp.sum(p2o, axis=-1, keepdims=True)
        r1 = 1.0 / l1
        r2 = lam / l2
        o = _dot((p1d * r1 - p2d * r2).astype(BF16), v_ref[qs:qs + TQ, :])
        if off1 is not None:
            o = o + _dot((p1o * r1 - p2o * r2).astype(BF16), v_ref[0:qs, :])
        o_ref[qs:qs + TQ, :] = (_rms(o, gsub_ref[...]) * (1.0 - lambda_init)).astype(o_ref.dtype)


def _diff_attn(u, lq1, lk1, lq2, lk2, g_sub, lambda_init, batch, seq):
    t = u.shape[0]
    part = lambda c: pl.BlockSpec((seq, LANES), lambda b, h: (b, c * HEADS + h))
    vec = pl.BlockSpec((1, LANES), lambda b, h: (0, 0))
    return pl.pallas_call(
        functools.partial(_diff_attn_kernel, seq=seq, scale=HEAD_DIM ** -0.5, lambda_init=lambda_init),
        name="diff_attn",
        grid=(batch, HEADS),
        in_specs=[part(0), part(1), part(2), part(3),
                  pl.BlockSpec((seq, C_V_DIM), lambda b, h: (b, 2 * HEADS + h)),
                  vec, vec, vec, vec,
                  pl.BlockSpec((1, C_V_DIM), lambda b, h: (0, 0))],
        out_specs=pl.BlockSpec((seq, C_V_DIM), lambda b, h: (b, h)),
        out_shape=jax.ShapeDtypeStruct((t, HEADS * C_V_DIM), BF16),
        compiler_params=_params("parallel", "parallel"),
    )(u, u, u, u, u, lq1, lk1, lq2, lk2, g_sub)


def _band_bias_kernel(tab_ref, o_ref):
    hd = pl.program_id(0)
    qi = lax.broadcasted_iota(jnp.int32, (TQ, A_WIN), 0)
    kj = lax.broadcasted_iota(jnp.int32, (TQ, A_WIN), 1)
    idx = jnp.clip(qi - kj + N_PREV_CHUNKS * CHUNK, -REL_FUTURE, REL_PAST_CLIP) + REL_FUTURE
    qc, kc = _chunk_of(qi), _chunk_of(kj)
    valid = (kc >= qc) & (kc <= qc + N_PREV_CHUNKS)

    def body(tt, acc):
        return jnp.where(idx == tt, tab_ref[hd, tt], acc)

    bias = lax.fori_loop(0, REL_TABLE, body, jnp.zeros((TQ, A_WIN), F32))
    o_ref[0] = jnp.where(valid, bias, NEG_INF)


def _band_bias(rel_table):
    return pl.pallas_call(
        _band_bias_kernel,
        name="band_bias",
        grid=(HEADS,),
        in_specs=[pl.BlockSpec(memory_space=pltpu.SMEM)],
        out_specs=pl.BlockSpec((1, TQ, A_WIN), lambda h: (h, 0, 0)),
        out_shape=jax.ShapeDtypeStruct((HEADS, TQ, A_WIN), F32),
        compiler_params=_params("parallel"),
    )(rel_table)


def _band_attn_kernel(q_ref, k_ref, v_ref, bias_ref, o_ref, *, seq, scale):
    for qs in range(0, seq, TQ):
        ws = max(0, qs - N_PREV_CHUNKS * CHUNK)
        s = _dot_t(q_ref[qs:qs + TQ, :], k_ref[ws:qs + TQ, :]) * scale
        s = s + bias_ref[0, :, A_WIN - (qs + TQ - ws):]
        p = jnp.exp(s - jnp.max(s, axis=-1, keepdims=True))
        l = jnp.sum(p, axis=-1, keepdims=True)
        o = _dot(p.astype(BF16), v_ref[ws:qs + TQ, :])
        o_ref[qs:qs + TQ, :] = (o / l).astype(o_ref.dtype)


def _band_attn(u, bias, batch, seq):
    t = u.shape[0]
    part = lambda c: pl.BlockSpec((seq, LANES), lambda b, h: (b, c * HEADS + h))
    return pl.pallas_call(
        functools.partial(_band_attn_kernel, seq=seq, scale=HEAD_DIM ** -0.5),
        name="band_attn",
        grid=(batch, HEADS),
        in_specs=[part(0), part(1), part(2),
                  pl.BlockSpec((1, TQ, A_WIN), lambda b, h: (h, 0, 0))],
        out_specs=pl.BlockSpec((seq, LANES), lambda b, h: (b, h)),
        out_shape=jax.ShapeDtypeStruct((t, A_WIDTH), BF16),
        compiler_params=_params("parallel", "parallel"),
    )(u, u, u, bias)


def _out_proj_kernel(*refs, n_in):
    x_refs, (w_ref, g_ref, h_ref, o_ref) = refs[:n_in], refs[n_in:]
    mix = None
    k0 = 0
    for x_ref in x_refs:
        k1 = k0 + x_ref.shape[1]
        part = _dot(x_ref[...], w_ref[k0:k1, :])
        mix = part if mix is None else mix + part
        k0 = k1
    o_ref[...] = h_ref[...] + _rms(mix, g_ref[...])


def _out_proj(xs, w, g, h, layer):
    t = h.shape[0]
    return pl.pallas_call(
        functools.partial(_out_proj_kernel, n_in=len(xs)),
        name="out_proj",
        grid=(t // TM_PROJ,),
        in_specs=[pl.BlockSpec((TM_PROJ, x.shape[1]), lambda i: (i, 0)) for x in xs] + [
            pl.BlockSpec((None,) + w.shape[1:], lambda i: (layer, 0, 0), pipeline_mode=pl.Buffered(1)),
            pl.BlockSpec((1, D_MODEL), lambda i: (0, 0)),
            pl.BlockSpec((TM_PROJ, D_MODEL), lambda i: (i, 0)),
        ],
        out_specs=pl.BlockSpec((TM_PROJ, D_MODEL), lambda i: (i, 0)),
        out_shape=jax.ShapeDtypeStruct((t, D_MODEL), F32),
        compiler_params=_params("parallel"),
    )(*xs, w, g, h)


def _ple_kernel(h_ref, gpre_ref, wg_ref, p_ref, wp_ref, gpost_ref, o_ref):
    h = h_ref[...]
    gate = jax.nn.sigmoid(_dot(_rms(h, gpre_ref[...]).astype(BF16), wg_ref[...]))
    emb = _dot(p_ref[...].astype(BF16), wp_ref[...])
    o_ref[...] = h + _rms(gate * emb, gpost_ref[...])


def _ple(h, g_pre, w_gate, p, w_proj, g_post, layer):
    t = h.shape[0]
    return pl.pallas_call(
        _ple_kernel,
        name="gated_embed",
        grid=(t // TM_PROJ,),
        in_specs=[
            pl.BlockSpec((TM_PROJ, D_MODEL), lambda i: (i, 0)),
            pl.BlockSpec((1, D_MODEL), lambda i: (0, 0)),
            pl.BlockSpec((None,) + w_gate.shape[1:], lambda i: (layer, 0, 0), pipeline_mode=pl.Buffered(1)),
            pl.BlockSpec((None, TM_PROJ, PLE_DIM), lambda i: (layer, i, 0)),
            pl.BlockSpec((None,) + w_proj.shape[1:], lambda i: (layer, 0, 0), pipeline_mode=pl.Buffered(1)),
            pl.BlockSpec((1, D_MODEL), lambda i: (0, 0)),
        ],
        out_specs=pl.BlockSpec((TM_PROJ, D_MODEL), lambda i: (i, 0)),
        out_shape=jax.ShapeDtypeStruct((t, D_MODEL), F32),
        compiler_params=_params("parallel"),
    )(h, g_pre, w_gate, p, w_proj, g_post)


def _rope_tables(seq, rot_dim):
    inv = ROPE_THETA ** (-jnp.arange(0, rot_dim, 2, dtype=F32) / rot_dim)
    ang = jnp.arange(seq, dtype=F32)[:, None] * inv[None, :]
    return jnp.cos(ang), jnp.sin(ang)


def _lane_tables(cos, sin, tail):
    seq, half = cos.shape
    rest = LANES - 2 * half
    zeros = lambda n: jnp.zeros((seq, n), F32)
    cos_t = jnp.concatenate([cos, cos, jnp.full((seq, rest), tail, F32)], axis=1)
    sin_up = jnp.concatenate([zeros(half), sin, zeros(rest)], axis=1)
    sin_dn = jnp.concatenate([-sin, zeros(LANES - half)], axis=1)
    return cos_t, sin_up, sin_dn


def _row(v):
    return v.reshape(1, -1)


def kernel(x, p, ffn1_g_pre, ffn1_w_in, ffn1_w_out, ffn1_g_post, mix_g_pre, mix_g_post, ab_w_in, a_rel_bias, b_g_q, b_w_qup, b_g_kv, b_w_kvup, ab_w_out, c_w_in, c_lq1, c_lk1, c_lq2, c_lk2, c_g_sub, c_w_out, ffn2_g_pre, ffn2_w_in, ffn2_w_out, ffn2_g_post, ple_g_pre, ple_w_gate, ple_w_proj, ple_g_post):
    batch, seq, _ = x.shape
    t = batch * seq
    assert seq % TQ == 0 and seq % TM_IN == 0 and seq % TM_PROJ == 0 and t % TM_FFN == 0

    tab_b = _lane_tables(*_rope_tables(seq, B_ROPE_DIM), tail=0.0)
    tab_c = _lane_tables(*_rope_tables(seq, C_ROT_DIM), tail=1.0)

    ffn1_w_in, ffn1_w_out, ffn2_w_in, ffn2_w_out, ple_w_gate, ple_w_proj, ab_w_out, c_w_out = (
        w.astype(BF16) for w in (ffn1_w_in, ffn1_w_out, ffn2_w_in, ffn2_w_out, ple_w_gate, ple_w_proj,
                                 ab_w_out, c_w_out))

    h = x.reshape(t, D_MODEL)
    p = p.reshape(DEPTH, t, PLE_DIM)
    for i in range(DEPTH):
        j = i // 2
        h = _ffn(h, _row(ffn1_g_pre[i]), ffn1_w_in, ffn1_w_out, _row(ffn1_g_post[i]), i)
        g_mix = _row(mix_g_pre[i])
        if i % 2 == 0:
            w_in = ab_w_in[j].astype(BF16)
            u_a = _proj(h, g_mix, w_in[:, :3 * A_WIDTH], tab_b, seq, rope_cols=0, half=0)
            w_b = jnp.pad(w_in[:, 3 * A_WIDTH:], ((0, 0), (0, LANES - B_ROPE_DIM)))
            w_q = b_w_qup[j].astype(BF16).reshape(B_Q_LORA, HEADS, B_NOPE_DIM + B_ROPE_DIM)
            w_q = jnp.pad(w_q, ((0, 0), (0, 0), (0, LANES - B_ROPE_DIM))).reshape(B_Q_LORA, -1)
            q_b, k_b, v_b = _mla_proj(h, g_mix, w_b, _row(b_g_q[j]), w_q, _row(b_g_kv[j]),
                                      b_w_kvup[j].astype(BF16), tab_b, seq)
            o_a = _band_attn(u_a, _band_bias(a_rel_bias[j]), batch, seq)
            o_b = _mla_attn(q_b, k_b, v_b, batch, seq)
            h = _out_proj([o_a, o_b], ab_w_out, _row(mix_g_post[i]), h, j)
        else:
            lambda_init = 0.8 - 0.6 * math.exp(-0.3 * i)
            w_c = c_w_in[j].astype(BF16).reshape(D_MODEL, HEADS, 4 * HEAD_DIM + C_V_DIM)
            w_qk = w_c[:, :, :4 * HEAD_DIM].reshape(D_MODEL, HEADS, 4, HEAD_DIM)
            w_qk = w_qk.transpose(0, 2, 1, 3).reshape(D_MODEL, 4 * A_WIDTH)
            w_c = jnp.concatenate([w_qk, w_c[:, :, 4 * HEAD_DIM:].reshape(D_MODEL, -1)], axis=1)
            u_c = _proj(h, g_mix, w_c, tab_c, seq, rope_cols=4 * A_WIDTH, half=C_ROT_DIM // 2)
            o_c = _diff_attn(u_c, _row(c_lq1[j]), _row(c_lk1[j]), _row(c_lq2[j]), _row(c_lk2[j]),
                             _row(c_g_sub[j]), lambda_init, batch, seq)
            h = _out_proj([o_c], c_w_out, _row(mix_g_post[i]), h, j)
        h = _ffn(h, _row(ffn2_g_pre[i]), ffn2_w_in, ffn2_w_out, _row(ffn2_g_post[i]), i)
        h = _ple(h, _row(ple_g_pre[i]), ple_w_gate, p, ple_w_proj, _row(ple_g_post[i]), i)
    return h.reshape(batch, seq, D_MODEL)
```

```python
import functools
import math

import jax
import jax.numpy as jnp
from jax import lax
from jax.experimental import pallas as pl
from jax.experimental.pallas import tpu as pltpu

D_MODEL = 2048
DEPTH = 4
CHUNK = 64
N_PREV_CHUNKS = 8
ROPE_THETA = 500000.0
PLE_DIM = 256
D_FF = 5632
NORM_EPS = 1e-6
NEG_INF = -1e30
LOG2E = math.log2(math.e)

HEADS = 8
HEAD_DIM = 128
A_WIDTH = HEADS * HEAD_DIM
REL_FUTURE = CHUNK - 1
REL_PAST_CLIP = 128
REL_TABLE = REL_FUTURE + REL_PAST_CLIP + 1

B_NOPE_DIM = 128
B_ROPE_DIM = 64
B_Q_LORA = 512
B_KV_LORA = 256

C_V_DIM = 2 * HEAD_DIM
C_ROT_DIM = HEAD_DIM // 4
C_HEAD_IN = 4 * HEAD_DIM + C_V_DIM

LANES = 128
VMEM_LIMIT = 58 * 1024 * 1024

TM_FFN = 1024
TF_FFN = 512
TM_IN = 1024
TM_PROJ = 512
TQ = 256
A_WIN = N_PREV_CHUNKS * CHUNK + TQ
BIAS_ROW = 1024

BF16 = jnp.bfloat16
F32 = jnp.float32


def _rms(x, g):
    return x * lax.rsqrt(jnp.mean(x * x, axis=-1, keepdims=True) + NORM_EPS) * g


def _dot(a, b):
    return jnp.dot(a, b, preferred_element_type=F32)


def _dot_t(a, b):
    return lax.dot_general(a, b, (((1,), (1,)), ((), ())), preferred_element_type=F32)


def _rope_group(u, cos, sin_up, sin_dn, half):
    return (u * cos + pltpu.roll(u, half, 1) * sin_up
            + pltpu.roll(u, LANES - half, 1) * sin_dn)


def _params(*sem):
    return pltpu.CompilerParams(dimension_semantics=sem, vmem_limit_bytes=VMEM_LIMIT)


def _ffn_kernel(h_ref, gpre_ref, wa_ref, wb_ref, wo_ref, gpost_ref, o_ref, hn_ref):
    j = pl.program_id(1)

    @pl.when(j == 0)
    def _():
        hn_ref[...] = _rms(h_ref[...], gpre_ref[...]).astype(BF16)
        o_ref[...] = jnp.zeros_like(o_ref)

    hn = hn_ref[...]
    a = _dot(hn, wa_ref[...])
    b = _dot(hn, wb_ref[...])
    g = (a * jax.nn.sigmoid(a) * b).astype(BF16)
    o_ref[...] += _dot(g, wo_ref[...])

    @pl.when(j == pl.num_programs(1) - 1)
    def _():
        o_ref[...] = h_ref[...] + 0.5 * _rms(o_ref[...], gpost_ref[...])


def _ffn(h, g_pre, w_in, w_out, g_post, layer):
    t = h.shape[0]
    nf = D_FF // TF_FFN
    return pl.pallas_call(
        _ffn_kernel,
        name="ffn",
        grid=(t // TM_FFN, nf),
        in_specs=[
            pl.BlockSpec((TM_FFN, D_MODEL), lambda i, j: (i, 0), pipeline_mode=pl.Buffered(1)),
            pl.BlockSpec((1, D_MODEL), lambda i, j: (0, 0)),
            pl.BlockSpec((None, D_MODEL, TF_FFN), lambda i, j: (layer, 0, j)),
            pl.BlockSpec((None, D_MODEL, TF_FFN), lambda i, j: (layer, 0, j + nf)),
            pl.BlockSpec((None, TF_FFN, D_MODEL), lambda i, j: (layer, j, 0)),
            pl.BlockSpec((1, D_MODEL), lambda i, j: (0, 0)),
        ],
        out_specs=pl.BlockSpec((TM_FFN, D_MODEL), lambda i, j: (i, 0)),
        out_shape=jax.ShapeDtypeStruct((t, D_MODEL), F32),
        scratch_shapes=[pltpu.VMEM((TM_FFN, D_MODEL), BF16)],
        compiler_params=_params("parallel", "arbitrary"),
    )(h, g_pre, w_in, w_in, w_out, g_post)


def _proj_kernel(h_ref, g_ref, w_ref, cos_ref, sup_ref, sdn_ref, o_ref, hn_ref, *, rope_groups, half):
    @pl.when(pl.program_id(1) == 0)
    def _():
        hn_ref[...] = _rms(h_ref[...], g_ref[...]).astype(BF16)

    u = _dot(hn_ref[...], w_ref[...])
    if not any(rope_groups):
        o_ref[...] = u.astype(o_ref.dtype)
        return
    cos, sup, sdn = cos_ref[...], sup_ref[...], sdn_ref[...]
    for c, rotate in enumerate(rope_groups):
        sl = slice(c * LANES, (c + 1) * LANES)
        x = _rope_group(u[:, sl], cos, sup, sdn, half) if rotate else u[:, sl]
        o_ref[:, sl] = x.astype(o_ref.dtype)


def _proj(h, g, w, layer, n_cols, tables, seq, rope_groups, half):
    t = h.shape[0]
    tn = len(rope_groups) * LANES
    blocks_per_seq = seq // TM_IN
    tab_spec = pl.BlockSpec((TM_IN, LANES), lambda i, j: (i % blocks_per_seq, 0))
    return pl.pallas_call(
        functools.partial(_proj_kernel, rope_groups=rope_groups, half=half),
        name="in_proj",
        grid=(t // TM_IN, n_cols // tn),
        in_specs=[
            pl.BlockSpec((TM_IN, D_MODEL), lambda i, j: (i, 0)),
            pl.BlockSpec((1, D_MODEL), lambda i, j: (0, 0)),
            pl.BlockSpec((None, D_MODEL, tn), lambda i, j: (layer, 0, j)),
            tab_spec, tab_spec, tab_spec,
        ],
        out_specs=pl.BlockSpec((TM_IN, tn), lambda i, j: (i, j)),
        out_shape=jax.ShapeDtypeStruct((t, n_cols), BF16),
        scratch_shapes=[pltpu.VMEM((TM_IN, D_MODEL), BF16)],
        compiler_params=_params("parallel", "arbitrary"),
    )(h, g, w, *tables)


def _mla_proj_kernel(h_ref, g_ref, wb_ref, gq_ref, wq_ref, gkv_ref, wkv_ref,
                     cos_ref, sup_ref, sdn_ref, q_ref, k_ref, v_ref):
    hn = _rms(h_ref[...], g_ref[...]).astype(BF16)
    u = _dot(hn, wb_ref[...])
    cq = u[:, :B_Q_LORA]
    ckv = u[:, B_Q_LORA:B_Q_LORA + B_KV_LORA]
    kr = u[:, B_Q_LORA + B_KV_LORA:]
    cos, sup, sdn = cos_ref[...], sup_ref[...], sdn_ref[...]
    half = B_ROPE_DIM // 2

    q = _dot(_rms(cq, gq_ref[...]).astype(BF16), wq_ref[...])
    kv = _dot(_rms(ckv, gkv_ref[...]).astype(BF16), wkv_ref[...])
    k_rope = _rope_group(kr, cos, sup, sdn, half).astype(BF16)
    for hd in range(HEADS):
        lo, mid, hi = 2 * hd * LANES, (2 * hd + 1) * LANES, (2 * hd + 2) * LANES
        q_ref[:, lo:mid] = q[:, lo:mid].astype(BF16)
        q_ref[:, mid:hi] = _rope_group(q[:, mid:hi], cos, sup, sdn, half).astype(BF16)
        k_ref[:, lo:mid] = kv[:, lo:mid].astype(BF16)
        k_ref[:, mid:hi] = k_rope
        v_ref[:, hd * LANES:(hd + 1) * LANES] = kv[:, mid:hi].astype(BF16)


def _mla_proj(h, g, wb, gq, wq, gkv, wkv, tables, seq):
    t = h.shape[0]
    blocks_per_seq = seq // TM_PROJ
    tab_spec = pl.BlockSpec((TM_PROJ, LANES), lambda i: (i % blocks_per_seq, 0))
    full = lambda a: pl.BlockSpec(a.shape, lambda i: (0, 0))
    wide = HEADS * 2 * LANES
    return pl.pallas_call(
        _mla_proj_kernel,
        name="mla_proj",
        grid=(t // TM_PROJ,),
        in_specs=[pl.BlockSpec((TM_PROJ, D_MODEL), lambda i: (i, 0)),
                  full(g), full(wb), full(gq), full(wq), full(gkv), full(wkv),
                  tab_spec, tab_spec, tab_spec],
        out_specs=[pl.BlockSpec((TM_PROJ, wide), lambda i: (i, 0)),
                   pl.BlockSpec((TM_PROJ, wide), lambda i: (i, 0)),
                   pl.BlockSpec((TM_PROJ, A_WIDTH), lambda i: (i, 0))],
        out_shape=[jax.ShapeDtypeStruct((t, wide), BF16),
                   jax.ShapeDtypeStruct((t, wide), BF16),
                   jax.ShapeDtypeStruct((t, A_WIDTH), BF16)],
        compiler_params=_params("parallel"),
    )(h, g, wb, gq, wq, gkv, wkv, *tables)


def _chunk_of(pos):
    return jnp.right_shift(pos, CHUNK.bit_length() - 1)


def _chunk_causal_mask():
    qc = _chunk_of(lax.broadcasted_iota(jnp.int32, (TQ, TQ), 0))
    kc = _chunk_of(lax.broadcasted_iota(jnp.int32, (TQ, TQ), 1))
    return kc <= qc


def _causal_weights(s, qs, mask, c):
    s_diag = jnp.where(mask, s[:, qs:], NEG_INF)
    m = jnp.max(s_diag, axis=-1, keepdims=True)
    if qs == 0:
        return [jnp.exp2((s_diag - m) * c)]
    s_off = s[:, :qs]
    m = jnp.maximum(m, jnp.max(s_off, axis=-1, keepdims=True))
    return [jnp.exp2((s_off - m) * c), jnp.exp2((s_diag - m) * c)]


def _as_bf16_row(parts):
    parts = [x.astype(BF16) for x in parts]
    return parts[0] if len(parts) == 1 else jnp.concatenate(parts, axis=1)


def _fill_value_and_ones(vaug_ref, v_ref):
    vaug_ref[:, :LANES] = v_ref[...]
    vaug_ref[:, LANES:] = jnp.ones((v_ref.shape[0], LANES), BF16)


def _mla_attn_kernel(q_ref, k_ref, v_ref, o_ref, vaug_ref, *, seq, scale):
    mask = _chunk_causal_mask()
    _fill_value_and_ones(vaug_ref, v_ref)
    for qs in range(0, seq, TQ):
        qe = qs + TQ
        s = _dot_t(q_ref[qs:qe, :], k_ref[0:qe, :])
        p = _as_bf16_row(_causal_weights(s, qs, mask, scale * LOG2E))
        o = _dot(p, vaug_ref[0:qe, :])
        o_ref[qs:qe, :] = (o[:, :LANES] / o[:, LANES:]).astype(o_ref.dtype)


def _mla_attn(q, k, v, batch, seq):
    t = q.shape[0]
    scale = (B_NOPE_DIM + B_ROPE_DIM) ** -0.5
    return pl.pallas_call(
        functools.partial(_mla_attn_kernel, seq=seq, scale=scale),
        name="mla_attn",
        grid=(batch, HEADS),
        in_specs=[pl.BlockSpec((seq, 2 * LANES), lambda b, h: (b, h)),
                  pl.BlockSpec((seq, 2 * LANES), lambda b, h: (b, h)),
                  pl.BlockSpec((seq, LANES), lambda b, h: (b, h))],
        out_specs=pl.BlockSpec((seq, LANES), lambda b, h: (b, h)),
        out_shape=jax.ShapeDtypeStruct((t, A_WIDTH), BF16),
        scratch_shapes=[pltpu.VMEM((seq, 2 * LANES), BF16)],
        compiler_params=_params("parallel", "parallel"),
    )(q, k, v)


def _diff_attn_kernel(q1_ref, q2_ref, k1_ref, k2_ref, v_ref, lq1_ref, lk1_ref, lq2_ref, lk2_ref,
                      gsub_ref, o_ref, *, seq, scale, lambda_init):
    mask = _chunk_causal_mask()
    lam = (jnp.exp(jnp.sum(lq1_ref[...] * lk1_ref[...], axis=-1, keepdims=True))
           - jnp.exp(jnp.sum(lq2_ref[...] * lk2_ref[...], axis=-1, keepdims=True)) + lambda_init)
    c = scale * LOG2E
    for qs in range(0, seq, TQ):
        qe = qs + TQ
        w1 = _causal_weights(_dot_t(q1_ref[qs:qe, :], k1_ref[0:qe, :]), qs, mask, c)
        w2 = _causal_weights(_dot_t(q2_ref[qs:qe, :], k2_ref[0:qe, :]), qs, mask, c)
        l1 = sum(jnp.sum(x, axis=-1, keepdims=True) for x in w1)
        l2 = sum(jnp.sum(x, axis=-1, keepdims=True) for x in w2)
        o12 = _dot(jnp.concatenate([_as_bf16_row(w1), _as_bf16_row(w2)], axis=0), v_ref[0:qe, :])
        o = o12[:TQ] * (1.0 / l1) - o12[TQ:] * (lam / l2)
        o_ref[qs:qe, :] = (_rms(o, gsub_ref[...]) * (1.0 - lambda_init)).astype(o_ref.dtype)


def _diff_attn(u, lq1, lk1, lq2, lk2, g_sub, lambda_init, batch, seq):
    t = u.shape[0]
    groups = C_HEAD_IN // LANES
    part = lambda c: pl.BlockSpec((seq, LANES), lambda b, h: (b, groups * h + c))
    vec = pl.BlockSpec((1, LANES), lambda b, h: (0, 0))
    return pl.pallas_call(
        functools.partial(_diff_attn_kernel, seq=seq, scale=HEAD_DIM ** -0.5, lambda_init=lambda_init),
        name="diff_attn",
        grid=(batch, HEADS),
        in_specs=[part(0), part(1), part(2), part(3),
                  pl.BlockSpec((seq, C_V_DIM), lambda b, h: (b, (C_HEAD_IN // C_V_DIM) * h + 2)),
                  vec, vec, vec, vec,
                  pl.BlockSpec((1, C_V_DIM), lambda b, h: (0, 0))],
        out_specs=pl.BlockSpec((seq, C_V_DIM), lambda b, h: (b, h)),
        out_shape=jax.ShapeDtypeStruct((t, HEADS * C_V_DIM), BF16),
        compiler_params=_params("parallel", "parallel"),
    )(u, u, u, u, u, lq1, lk1, lq2, lk2, g_sub)


def _band_bias_kernel(tab_ref, o_ref, *, inv_scale):
    hd = pl.program_id(0)
    first = N_PREV_CHUNKS * CHUNK
    kj = lax.broadcasted_iota(jnp.int32, (8, BIAS_ROW), 1)
    dist = jnp.where(kj < A_WIN, first - kj, first + BIAS_ROW - kj)
    idx = jnp.clip(dist, -REL_FUTURE, REL_PAST_CLIP) + REL_FUTURE

    def body(tt, acc):
        return jnp.where(idx == tt, tab_ref[hd, tt], acc)

    row = lax.fori_loop(0, REL_TABLE, body, jnp.zeros((8, BIAS_ROW), F32))
    bias = jnp.broadcast_to(row[0:1], (TQ, BIAS_ROW))
    qi = lax.broadcasted_iota(jnp.int32, (TQ, BIAS_ROW), 0)
    for bit in range(TQ.bit_length() - 1):
        rotated = pltpu.roll(bias, 1 << bit, 1)
        bias = jnp.where((jnp.right_shift(qi, bit) & 1) == 1, rotated, bias)
    qc = _chunk_of(lax.broadcasted_iota(jnp.int32, (TQ, A_WIN), 0))
    kc = _chunk_of(lax.broadcasted_iota(jnp.int32, (TQ, A_WIN), 1))
    valid = (kc >= qc) & (kc <= qc + N_PREV_CHUNKS)
    o_ref[0] = jnp.where(valid, bias[:, :A_WIN] * inv_scale, NEG_INF)


def _band_bias(rel_table, scale):
    assert BIAS_ROW >= A_WIN + TQ
    return pl.pallas_call(
        functools.partial(_band_bias_kernel, inv_scale=1.0 / scale),
        name="band_bias",
        grid=(HEADS,),
        in_specs=[pl.BlockSpec(memory_space=pltpu.SMEM)],
        out_specs=pl.BlockSpec((1, TQ, A_WIN), lambda h: (h, 0, 0)),
        out_shape=jax.ShapeDtypeStruct((HEADS, TQ, A_WIN), F32),
        compiler_params=_params("parallel"),
    )(rel_table)


def _band_attn_kernel(q_ref, k_ref, v_ref, bias_ref, o_ref, vaug_ref, *, seq, scale):
    _fill_value_and_ones(vaug_ref, v_ref)
    for qs in range(0, seq, TQ):
        qe = qs + TQ
        ws = max(0, qs - N_PREV_CHUNKS * CHUNK)
        s = _dot_t(q_ref[qs:qe, :], k_ref[ws:qe, :]) + bias_ref[0, :, A_WIN - (qe - ws):]
        p = jnp.exp2((s - jnp.max(s, axis=-1, keepdims=True)) * (scale * LOG2E))
        o = _dot(p.astype(BF16), vaug_ref[ws:qe, :])
        o_ref[qs:qe, :] = (o[:, :LANES] / o[:, LANES:]).astype(o_ref.dtype)


def _band_attn(u, rel_table, batch, seq):
    t = u.shape[0]
    scale = HEAD_DIM ** -0.5
    part = lambda c: pl.BlockSpec((seq, LANES), lambda b, h: (b, c * HEADS + h))
    return pl.pallas_call(
        functools.partial(_band_attn_kernel, seq=seq, scale=scale),
        name="band_attn",
        grid=(batch, HEADS),
        in_specs=[part(0), part(1), part(2),
                  pl.BlockSpec((1, TQ, A_WIN), lambda b, h: (h, 0, 0))],
        out_specs=pl.BlockSpec((seq, LANES), lambda b, h: (b, h)),
        out_shape=jax.ShapeDtypeStruct((t, A_WIDTH), BF16),
        scratch_shapes=[pltpu.VMEM((seq, 2 * LANES), BF16)],
        compiler_params=_params("parallel", "parallel"),
    )(u, u, u, _band_bias(rel_table, scale))


def _out_proj_kernel(*refs, n_in):
    x_refs, (w_ref, g_ref, h_ref, o_ref) = refs[:n_in], refs[n_in:]
    mix = None
    k0 = 0
    for x_ref in x_refs:
        k1 = k0 + x_ref.shape[1]
        part = _dot(x_ref[...], w_ref[k0:k1, :])
        mix = part if mix is None else mix + part
        k0 = k1
    o_ref[...] = h_ref[...] + _rms(mix, g_ref[...])


def _out_proj(xs, w, g, h, layer):
    t = h.shape[0]
    return pl.pallas_call(
        functools.partial(_out_proj_kernel, n_in=len(xs)),
        name="out_proj",
        grid=(t // TM_PROJ,),
        in_specs=[pl.BlockSpec((TM_PROJ, x.shape[1]), lambda i: (i, 0)) for x in xs] + [
            pl.BlockSpec((None,) + w.shape[1:], lambda i: (layer, 0, 0), pipeline_mode=pl.Buffered(1)),
            pl.BlockSpec((1, D_MODEL), lambda i: (0, 0)),
            pl.BlockSpec((TM_PROJ, D_MODEL), lambda i: (i, 0)),
        ],
        out_specs=pl.BlockSpec((TM_PROJ, D_MODEL), lambda i: (i, 0)),
        out_shape=jax.ShapeDtypeStruct((t, D_MODEL), F32),
        compiler_params=_params("parallel"),
    )(*xs, w, g, h)


def _ple_kernel(h_ref, gpre_ref, wg_ref, p_ref, wp_ref, gpost_ref, o_ref):
    h = h_ref[...]
    gate = jax.nn.sigmoid(_dot(_rms(h, gpre_ref[...]).astype(BF16), wg_ref[...]))
    emb = _dot(p_ref[...].astype(BF16), wp_ref[...])
    o_ref[...] = h + _rms(gate * emb, gpost_ref[...])


def _ple(h, g_pre, w_gate, p, w_proj, g_post, layer):
    t = h.shape[0]
    return pl.pallas_call(
        _ple_kernel,
        name="gated_embed",
        grid=(t // TM_PROJ,),
        in_specs=[
            pl.BlockSpec((TM_PROJ, D_MODEL), lambda i: (i, 0)),
            pl.BlockSpec((1, D_MODEL), lambda i: (0, 0)),
            pl.BlockSpec((None,) + w_gate.shape[1:], lambda i: (layer, 0, 0), pipeline_mode=pl.Buffered(1)),
            pl.BlockSpec((None, TM_PROJ, PLE_DIM), lambda i: (layer, i, 0)),
            pl.BlockSpec((None,) + w_proj.shape[1:], lambda i: (layer, 0, 0), pipeline_mode=pl.Buffered(1)),
            pl.BlockSpec((1, D_MODEL), lambda i: (0, 0)),
        ],
        out_specs=pl.BlockSpec((TM_PROJ, D_MODEL), lambda i: (i, 0)),
        out_shape=jax.ShapeDtypeStruct((t, D_MODEL), F32),
        compiler_params=_params("parallel"),
    )(h, g_pre, w_gate, p, w_proj, g_post)


def _rope_tables(seq, rot_dim):
    inv = ROPE_THETA ** (-jnp.arange(0, rot_dim, 2, dtype=F32) / rot_dim)
    ang = jnp.arange(seq, dtype=F32)[:, None] * inv[None, :]
    return jnp.cos(ang), jnp.sin(ang)


def _lane_tables(cos, sin, tail):
    seq, half = cos.shape
    rest = LANES - 2 * half
    zeros = lambda n: jnp.zeros((seq, n), F32)
    cos_t = jnp.concatenate([cos, cos, jnp.full((seq, rest), tail, F32)], axis=1)
    sin_up = jnp.concatenate([zeros(half), sin, zeros(rest)], axis=1)
    sin_dn = jnp.concatenate([-sin, zeros(LANES - half)], axis=1)
    return cos_t, sin_up, sin_dn


def _row(v):
    return v.reshape(1, -1)


def kernel(x, p, ffn1_g_pre, ffn1_w_in, ffn1_w_out, ffn1_g_post, mix_g_pre, mix_g_post, ab_w_in, a_rel_bias, b_g_q, b_w_qup, b_g_kv, b_w_kvup, ab_w_out, c_w_in, c_lq1, c_lk1, c_lq2, c_lk2, c_g_sub, c_w_out, ffn2_g_pre, ffn2_w_in, ffn2_w_out, ffn2_g_post, ple_g_pre, ple_w_gate, ple_w_proj, ple_g_post):
    batch, seq, _ = x.shape
    t = batch * seq
    assert seq % TQ == 0 and seq % TM_IN == 0 and seq % TM_PROJ == 0 and t % TM_FFN == 0

    tab_b = _lane_tables(*_rope_tables(seq, B_ROPE_DIM), tail=0.0)
    tab_c = _lane_tables(*_rope_tables(seq, C_ROT_DIM), tail=1.0)

    (ffn1_w_in, ffn1_w_out, ffn2_w_in, ffn2_w_out, ple_w_gate, ple_w_proj, ab_w_in, ab_w_out,
     c_w_in, c_w_out, b_w_qup, b_w_kvup) = (
        w.astype(BF16) for w in (ffn1_w_in, ffn1_w_out, ffn2_w_in, ffn2_w_out, ple_w_gate, ple_w_proj,
                                 ab_w_in, ab_w_out, c_w_in, c_w_out, b_w_qup, b_w_kvup))
    pad_rope = LANES - B_ROPE_DIM
    w_b = jnp.pad(ab_w_in[:, :, 3 * A_WIDTH:], ((0, 0), (0, 0), (0, pad_rope)))
    w_q = b_w_qup.reshape(-1, B_Q_LORA, HEADS, B_NOPE_DIM + B_ROPE_DIM)
    w_q = jnp.pad(w_q, ((0, 0), (0, 0), (0, 0), (0, pad_rope))).reshape(-1, B_Q_LORA, HEADS * 2 * LANES)

    plain = (False,) * 8
    c_groups = (True,) * 4 + (False,) * 2

    h = x.reshape(t, D_MODEL)
    p = p.reshape(DEPTH, t, PLE_DIM)
    for i in range(DEPTH):
        j = i // 2
        h = _ffn(h, _row(ffn1_g_pre[i]), ffn1_w_in, ffn1_w_out, _row(ffn1_g_post[i]), i)
        g_mix = _row(mix_g_pre[i])
        if i % 2 == 0:
            u_a = _proj(h, g_mix, ab_w_in, j, 3 * A_WIDTH, tab_b, seq, plain, 0)
            q_b, k_b, v_b = _mla_proj(h, g_mix, w_b[j], _row(b_g_q[j]), w_q[j], _row(b_g_kv[j]),
                                      b_w_kvup[j], tab_b, seq)
            o_a = _band_attn(u_a, a_rel_bias[j], batch, seq)
            o_b = _mla_attn(q_b, k_b, v_b, batch, seq)
            h = _out_proj([o_a, o_b], ab_w_out, _row(mix_g_post[i]), h, j)
        else:
            lambda_init = 0.8 - 0.6 * math.exp(-0.3 * i)
            u_c = _proj(h, g_mix, c_w_in, j, HEADS * C_HEAD_IN, tab_c, seq, c_groups, C_ROT_DIM // 2)
            o_c = _diff_attn(u_c, _row(c_lq1[j]), _row(c_lk1[j]), _row(c_lq2[j]), _row(c_lk2[j]),
                             _row(c_g_sub[j]), lambda_init, batch, seq)
            h = _out_proj([o_c], c_w_out, _row(mix_g_post[i]), h, j)
        h = _ffn(h, _row(ffn2_g_pre[i]), ffn2_w_in, ffn2_w_out, _row(ffn2_g_post[i]), i)
        h = _ple(h, _row(ple_g_pre[i]), ple_w_gate, p, ple_w_proj, _row(ple_g_post[i]), i)
    return h.reshape(batch, seq, D_MODEL)
```

```python
import functools
import math

import jax
import jax.numpy as jnp
from jax import lax
from jax.experimental import pallas as pl
from jax.experimental.pallas import tpu as pltpu

D_MODEL = 2048
DEPTH = 4
CHUNK = 64
N_PREV_CHUNKS = 8
ROPE_THETA = 500000.0
PLE_DIM = 256
D_FF = 5632
NORM_EPS = 1e-6
NEG_INF = -1e30
LOG2E = math.log2(math.e)

HEADS = 8
HEAD_DIM = 128
A_WIDTH = HEADS * HEAD_DIM
REL_FUTURE = CHUNK - 1
REL_PAST_CLIP = 128
REL_TABLE = REL_FUTURE + REL_PAST_CLIP + 1

B_NOPE_DIM = 128
B_ROPE_DIM = 64
B_Q_LORA = 512
B_KV_LORA = 256

C_V_DIM = 2 * HEAD_DIM
C_ROT_DIM = HEAD_DIM // 4
C_HEAD_IN = 4 * HEAD_DIM + C_V_DIM

LANES = 128
VMEM_LIMIT = 58 * 1024 * 1024

FFN_VMEM_LIMIT = 62 * 1024 * 1024

TM_FFN = 1024
TF_FFN = 512
FFN_ROW_CHUNKS = 4
TM_IN = 1024
TM_PROJ = 512
TQ = 256
A_WIN = N_PREV_CHUNKS * CHUNK + TQ
BIAS_ROW = 1024

BF16 = jnp.bfloat16
F32 = jnp.float32


def _rms(x, g):
    return x * lax.rsqrt(jnp.mean(x * x, axis=-1, keepdims=True) + NORM_EPS) * g


def _dot(a, b):
    return jnp.dot(a, b, preferred_element_type=F32)


def _dot_t(a, b):
    return lax.dot_general(a, b, (((1,), (1,)), ((), ())), preferred_element_type=F32)


def _rope_group(u, cos, sin_up, sin_dn, half):
    return (u * cos + pltpu.roll(u, half, 1) * sin_up
            + pltpu.roll(u, LANES - half, 1) * sin_dn)


def _params(*sem):
    return pltpu.CompilerParams(dimension_semantics=sem, vmem_limit_bytes=VMEM_LIMIT)


def _ffn_kernel(*refs, cast_next):
    if cast_next:
        (h_ref, gpre_ref, wa_ref, wb_ref, wo_ref, gpost_ref, nin_ref, nout_ref,
         o_ref, nin_bf_ref, nout_bf_ref, hn_ref) = refs
        nin_bf_ref[...] = nin_ref[...].astype(BF16)
        nout_bf_ref[...] = nout_ref[...].astype(BF16)
    else:
        h_ref, gpre_ref, wa_ref, wb_ref, wo_ref, gpost_ref, o_ref, hn_ref = refs
    j = pl.program_id(1)
    last = pl.num_programs(1) - 1
    chunks = [slice(r, r + TM_FFN // FFN_ROW_CHUNKS) for r in range(0, TM_FFN, TM_FFN // FFN_ROW_CHUNKS)]

    def swiglu(rows):
        hn = hn_ref[rows, :]
        a = _dot(hn, wa_ref[...])
        b = _dot(hn, wb_ref[...])
        return _dot((a * jax.nn.sigmoid(a) * b).astype(BF16), wo_ref[...])

    @pl.when(j == 0)
    def _():
        for rows in chunks:
            hn_ref[rows, :] = _rms(h_ref[rows, :], gpre_ref[...]).astype(BF16)
            o_ref[rows, :] = swiglu(rows)

    @pl.when((j > 0) & (j < last))
    def _():
        o_ref[...] += swiglu(slice(None))

    @pl.when(j == last)
    def _():
        for rows in chunks:
            acc = o_ref[rows, :] + swiglu(rows)
            o_ref[rows, :] = h_ref[rows, :] + 0.5 * _rms(acc, gpost_ref[...])


def _ffn(h, g_pre, w_in, w_out, g_post, cast_next=None):
    t = h.shape[0]
    ni, nf = t // TM_FFN, D_FF // TF_FFN
    assert nf >= 3
    in_specs = [
        pl.BlockSpec((TM_FFN, D_MODEL), lambda i, j: (i, 0)),
        pl.BlockSpec((1, D_MODEL), lambda i, j: (0, 0)),
        pl.BlockSpec((D_MODEL, TF_FFN), lambda i, j: (0, j)),
        pl.BlockSpec((D_MODEL, TF_FFN), lambda i, j: (0, j + nf)),
        pl.BlockSpec((TF_FFN, D_MODEL), lambda i, j: (j, 0)),
        pl.BlockSpec((1, D_MODEL), lambda i, j: (0, 0)),
    ]
    out_specs = [pl.BlockSpec((TM_FFN, D_MODEL), lambda i, j: (i, 0))]
    out_shape = [jax.ShapeDtypeStruct((t, D_MODEL), F32)]
    args = [h, g_pre, w_in, w_in, w_out, g_post]
    if cast_next is not None:
        nxt_in, nxt_out, layer = cast_next
        in_tile = (D_MODEL // ni, 2 * D_FF // nf)
        out_tile = (D_FF // nf, D_MODEL // ni)
        in_specs += [pl.BlockSpec((None,) + in_tile, lambda i, j: (layer, i, j)),
                     pl.BlockSpec((None,) + out_tile, lambda i, j: (layer, j, i))]
        out_specs += [pl.BlockSpec(in_tile, lambda i, j: (i, j)),
                      pl.BlockSpec(out_tile, lambda i, j: (j, i))]
        out_shape += [jax.ShapeDtypeStruct(nxt_in.shape[1:], BF16),
                      jax.ShapeDtypeStruct(nxt_out.shape[1:], BF16)]
        args += [nxt_in, nxt_out]
    return pl.pallas_call(
        functools.partial(_ffn_kernel, cast_next=cast_next is not None),
        name="ffn",
        grid=(ni, nf),
        in_specs=in_specs,
        out_specs=out_specs,
        out_shape=out_shape,
        scratch_shapes=[pltpu.VMEM((TM_FFN, D_MODEL), BF16)],
        compiler_params=pltpu.CompilerParams(dimension_semantics=("parallel", "arbitrary"),
                                             vmem_limit_bytes=FFN_VMEM_LIMIT),
    )(*args)


def _proj_kernel(h_ref, g_ref, w_ref, cos_ref, sup_ref, sdn_ref, o_ref, hn_ref, *, rope_groups, half):
    @pl.when(pl.program_id(1) == 0)
    def _():
        hn_ref[...] = _rms(h_ref[...], g_ref[...]).astype(BF16)

    u = _dot(hn_ref[...], w_ref[...])
    if not any(rope_groups):
        o_ref[...] = u.astype(o_ref.dtype)
        return
    cos, sup, sdn = cos_ref[...], sup_ref[...], sdn_ref[...]
    for c, rotate in enumerate(rope_groups):
        sl = slice(c * LANES, (c + 1) * LANES)
        x = _rope_group(u[:, sl], cos, sup, sdn, half) if rotate else u[:, sl]
        o_ref[:, sl] = x.astype(o_ref.dtype)


def _proj(h, g, w, layer, n_cols, tables, seq, rope_groups, half):
    t = h.shape[0]
    tn = len(rope_groups) * LANES
    blocks_per_seq = seq // TM_IN
    tab_spec = pl.BlockSpec((TM_IN, LANES), lambda i, j: (i % blocks_per_seq, 0))
    return pl.pallas_call(
        functools.partial(_proj_kernel, rope_groups=rope_groups, half=half),
        name="in_proj",
        grid=(t // TM_IN, n_cols // tn),
        in_specs=[
            pl.BlockSpec((TM_IN, D_MODEL), lambda i, j: (i, 0)),
            pl.BlockSpec((1, D_MODEL), lambda i, j: (0, 0)),
            pl.BlockSpec((None, D_MODEL, tn), lambda i, j: (layer, 0, j)),
            tab_spec, tab_spec, tab_spec,
        ],
        out_specs=pl.BlockSpec((TM_IN, tn), lambda i, j: (i, j)),
        out_shape=jax.ShapeDtypeStruct((t, n_cols), BF16),
        scratch_shapes=[pltpu.VMEM((TM_IN, D_MODEL), BF16)],
        compiler_params=_params("parallel", "arbitrary"),
    )(h, g, w, *tables)


def _mla_proj_kernel(h_ref, g_ref, wb_ref, gq_ref, wq_ref, gkv_ref, wkv_ref,
                     cos_ref, sup_ref, sdn_ref, q_ref, k_ref, v_ref):
    hn = _rms(h_ref[...], g_ref[...]).astype(BF16)
    u = _dot(hn, wb_ref[...])
    cq = u[:, :B_Q_LORA]
    ckv = u[:, B_Q_LORA:B_Q_LORA + B_KV_LORA]
    kr = u[:, B_Q_LORA + B_KV_LORA:]
    cos, sup, sdn = cos_ref[...], sup_ref[...], sdn_ref[...]
    half = B_ROPE_DIM // 2

    q = _dot(_rms(cq, gq_ref[...]).astype(BF16), wq_ref[...])
    kv = _dot(_rms(ckv, gkv_ref[...]).astype(BF16), wkv_ref[...])
    k_rope = _rope_group(kr, cos, sup, sdn, half).astype(BF16)
    for hd in range(HEADS):
        lo, mid, hi = 2 * hd * LANES, (2 * hd + 1) * LANES, (2 * hd + 2) * LANES
        q_ref[:, lo:mid] = q[:, lo:mid].astype(BF16)
        q_ref[:, mid:hi] = _rope_group(q[:, mid:hi], cos, sup, sdn, half).astype(BF16)
        k_ref[:, lo:mid] = kv[:, lo:mid].astype(BF16)
        k_ref[:, mid:hi] = k_rope
        v_ref[:, hd * LANES:(hd + 1) * LANES] = kv[:, mid:hi].astype(BF16)


def _mla_proj(h, g, wb, gq, wq, gkv, wkv, tables, seq):
    t = h.shape[0]
    blocks_per_seq = seq // TM_PROJ
    tab_spec = pl.BlockSpec((TM_PROJ, LANES), lambda i: (i % blocks_per_seq, 0))
    full = lambda a: pl.BlockSpec(a.shape, lambda i: (0, 0))
    wide = HEADS * 2 * LANES
    return pl.pallas_call(
        _mla_proj_kernel,
        name="mla_proj",
        grid=(t // TM_PROJ,),
        in_specs=[pl.BlockSpec((TM_PROJ, D_MODEL), lambda i: (i, 0)),
                  full(g), full(wb), full(gq), full(wq), full(gkv), full(wkv),
                  tab_spec, tab_spec, tab_spec],
        out_specs=[pl.BlockSpec((TM_PROJ, wide), lambda i: (i, 0)),
                   pl.BlockSpec((TM_PROJ, wide), lambda i: (i, 0)),
                   pl.BlockSpec((TM_PROJ, A_WIDTH), lambda i: (i, 0))],
        out_shape=[jax.ShapeDtypeStruct((t, wide), BF16),
                   jax.ShapeDtypeStruct((t, wide), BF16),
                   jax.ShapeDtypeStruct((t, A_WIDTH), BF16)],
        compiler_params=_params("parallel"),
    )(h, g, wb, gq, wq, gkv, wkv, *tables)


def _chunk_of(pos):
    return jnp.right_shift(pos, CHUNK.bit_length() - 1)


def _chunk_causal_mask():
    qc = _chunk_of(lax.broadcasted_iota(jnp.int32, (TQ, TQ), 0))
    kc = _chunk_of(lax.broadcasted_iota(jnp.int32, (TQ, TQ), 1))
    return kc <= qc


def _causal_weights(s, qs, mask, c):
    s_diag = jnp.where(mask, s[:, qs:], NEG_INF)
    m = jnp.max(s_diag, axis=-1, keepdims=True)
    if qs == 0:
        return [jnp.exp2((s_diag - m) * c)]
    s_off = s[:, :qs]
    m = jnp.maximum(m, jnp.max(s_off, axis=-1, keepdims=True))
    return [jnp.exp2((s_off - m) * c), jnp.exp2((s_diag - m) * c)]


def _as_bf16_row(parts):
    parts = [x.astype(BF16) for x in parts]
    return parts[0] if len(parts) == 1 else jnp.concatenate(parts, axis=1)


def _fill_value_and_ones(vaug_ref, v_ref):
    vaug_ref[:, :LANES] = v_ref[...]
    vaug_ref[:, LANES:] = jnp.ones((v_ref.shape[0], LANES), BF16)


def _mla_attn_kernel(q_ref, k_ref, v_ref, o_ref, vaug_ref, *, seq, scale):
    mask = _chunk_causal_mask()
    _fill_value_and_ones(vaug_ref, v_ref)
    for qs in range(0, seq, TQ):
        qe = qs + TQ
        s = _dot_t(q_ref[qs:qe, :], k_ref[0:qe, :])
        p = _as_bf16_row(_causal_weights(s, qs, mask, scale * LOG2E))
        o = _dot(p, vaug_ref[0:qe, :])
        o_ref[qs:qe, :] = (o[:, :LANES] / o[:, LANES:]).astype(o_ref.dtype)


def _mla_attn(q, k, v, batch, seq):
    t = q.shape[0]
    scale = (B_NOPE_DIM + B_ROPE_DIM) ** -0.5
    return pl.pallas_call(
        functools.partial(_mla_attn_kernel, seq=seq, scale=scale),
        name="mla_attn",
        grid=(batch, HEADS),
        in_specs=[pl.BlockSpec((seq, 2 * LANES), lambda b, h: (b, h)),
                  pl.BlockSpec((seq, 2 * LANES), lambda b, h: (b, h)),
                  pl.BlockSpec((seq, LANES), lambda b, h: (b, h))],
        out_specs=pl.BlockSpec((seq, LANES), lambda b, h: (b, h)),
        out_shape=jax.ShapeDtypeStruct((t, A_WIDTH), BF16),
        scratch_shapes=[pltpu.VMEM((seq, 2 * LANES), BF16)],
        compiler_params=_params("parallel", "parallel"),
    )(q, k, v)


def _diff_attn_kernel(q1_ref, q2_ref, k1_ref, k2_ref, v_ref, lq1_ref, lk1_ref, lq2_ref, lk2_ref,
                      gsub_ref, o_ref, *, seq, scale, lambda_init):
    mask = _chunk_causal_mask()
    lam = (jnp.exp(jnp.sum(lq1_ref[...] * lk1_ref[...], axis=-1, keepdims=True))
           - jnp.exp(jnp.sum(lq2_ref[...] * lk2_ref[...], axis=-1, keepdims=True)) + lambda_init)
    c = scale * LOG2E
    for qs in range(0, seq, TQ):
        qe = qs + TQ
        w1 = _causal_weights(_dot_t(q1_ref[qs:qe, :], k1_ref[0:qe, :]), qs, mask, c)
        w2 = _causal_weights(_dot_t(q2_ref[qs:qe, :], k2_ref[0:qe, :]), qs, mask, c)
        l1 = sum(jnp.sum(x, axis=-1, keepdims=True) for x in w1)
        l2 = sum(jnp.sum(x, axis=-1, keepdims=True) for x in w2)
        o12 = _dot(jnp.concatenate([_as_bf16_row(w1), _as_bf16_row(w2)], axis=0), v_ref[0:qe, :])
        o = o12[:TQ] * (1.0 / l1) - o12[TQ:] * (lam / l2)
        o_ref[qs:qe, :] = (_rms(o, gsub_ref[...]) * (1.0 - lambda_init)).astype(o_ref.dtype)


def _diff_attn(u, lq1, lk1, lq2, lk2, g_sub, lambda_init, batch, seq):
    t = u.shape[0]
    groups = C_HEAD_IN // LANES
    part = lambda c: pl.BlockSpec((seq, LANES), lambda b, h: (b, groups * h + c))
    vec = pl.BlockSpec((1, LANES), lambda b, h: (0, 0))
    return pl.pallas_call(
        functools.partial(_diff_attn_kernel, seq=seq, scale=HEAD_DIM ** -0.5, lambda_init=lambda_init),
        name="diff_attn",
        grid=(batch, HEADS),
        in_specs=[part(0), part(1), part(2), part(3),
                  pl.BlockSpec((seq, C_V_DIM), lambda b, h: (b, (C_HEAD_IN // C_V_DIM) * h + 2)),
                  vec, vec, vec, vec,
                  pl.BlockSpec((1, C_V_DIM), lambda b, h: (0, 0))],
        out_specs=pl.BlockSpec((seq, C_V_DIM), lambda b, h: (b, h)),
        out_shape=jax.ShapeDtypeStruct((t, HEADS * C_V_DIM), BF16),
        compiler_params=_params("parallel", "parallel"),
    )(u, u, u, u, u, lq1, lk1, lq2, lk2, g_sub)


def _band_bias_kernel(tab_ref, o_ref, *, inv_scale):
    hd = pl.program_id(0)
    first = N_PREV_CHUNKS * CHUNK
    kj = lax.broadcasted_iota(jnp.int32, (8, BIAS_ROW), 1)
    dist = jnp.where(kj < A_WIN, first - kj, first + BIAS_ROW - kj)
    idx = jnp.clip(dist, -REL_FUTURE, REL_PAST_CLIP) + REL_FUTURE

    def body(tt, acc):
        return jnp.where(idx == tt, tab_ref[hd, tt], acc)

    base = lax.fori_loop(0, REL_TABLE, body, jnp.zeros((8, BIAS_ROW), F32))
    sub = lax.broadcasted_iota(jnp.int32, (8, BIAS_ROW), 0)
    for bit in range(3):
        rotated = pltpu.roll(base, 1 << bit, 1)
        base = jnp.where((jnp.right_shift(sub, bit) & 1) == 1, rotated, base)
    bias = jnp.concatenate([pltpu.roll(base, 8 * b, 1) if b else base for b in range(TQ // 8)], axis=0)
    qc = _chunk_of(lax.broadcasted_iota(jnp.int32, (TQ, A_WIN), 0))
    kc = _chunk_of(lax.broadcasted_iota(jnp.int32, (TQ, A_WIN), 1))
    valid = (kc >= qc) & (kc <= qc + N_PREV_CHUNKS)
    o_ref[0] = jnp.where(valid, bias[:, :A_WIN] * inv_scale, NEG_INF)


def _band_bias(rel_table, scale):
    assert BIAS_ROW >= A_WIN + TQ
    return pl.pallas_call(
        functools.partial(_band_bias_kernel, inv_scale=1.0 / scale),
        name="band_bias",
        grid=(HEADS,),
        in_specs=[pl.BlockSpec(memory_space=pltpu.SMEM)],
        out_specs=pl.BlockSpec((1, TQ, A_WIN), lambda h: (h, 0, 0)),
        out_shape=jax.ShapeDtypeStruct((HEADS, TQ, A_WIN), F32),
        compiler_params=_params("parallel"),
    )(rel_table)


def _band_attn_kernel(q_ref, k_ref, v_ref, bias_ref, o_ref, vaug_ref, *, seq, scale):
    _fill_value_and_ones(vaug_ref, v_ref)
    for qs in range(0, seq, TQ):
        qe = qs + TQ
        ws = max(0, qs - N_PREV_CHUNKS * CHUNK)
        s = _dot_t(q_ref[qs:qe, :], k_ref[ws:qe, :]) + bias_ref[0, :, A_WIN - (qe - ws):]
        p = jnp.exp2((s - jnp.max(s, axis=-1, keepdims=True)) * (scale * LOG2E))
        o = _dot(p.astype(BF16), vaug_ref[ws:qe, :])
        o_ref[qs:qe, :] = (o[:, :LANES] / o[:, LANES:]).astype(o_ref.dtype)


def _band_attn(u, rel_table, batch, seq):
    t = u.shape[0]
    scale = HEAD_DIM ** -0.5
    part = lambda c: pl.BlockSpec((seq, LANES), lambda b, h: (b, c * HEADS + h))
    return pl.pallas_call(
        functools.partial(_band_attn_kernel, seq=seq, scale=scale),
        name="band_attn",
        grid=(batch, HEADS),
        in_specs=[part(0), part(1), part(2),
                  pl.BlockSpec((1, TQ, A_WIN), lambda b, h: (h, 0, 0))],
        out_specs=pl.BlockSpec((seq, LANES), lambda b, h: (b, h)),
        out_shape=jax.ShapeDtypeStruct((t, A_WIDTH), BF16),
        scratch_shapes=[pltpu.VMEM((seq, 2 * LANES), BF16)],
        compiler_params=_params("parallel", "parallel"),
    )(u, u, u, _band_bias(rel_table, scale))


def _out_proj_kernel(*refs, n_in):
    x_refs, (w_ref, g_ref, h_ref, o_ref) = refs[:n_in], refs[n_in:]
    mix = None
    k0 = 0
    for x_ref in x_refs:
        k1 = k0 + x_ref.shape[1]
        part = _dot(x_ref[...], w_ref[k0:k1, :])
        mix = part if mix is None else mix + part
        k0 = k1
    o_ref[...] = h_ref[...] + _rms(mix, g_ref[...])


def _out_proj(xs, w, g, h, layer):
    t = h.shape[0]
    return pl.pallas_call(
        functools.partial(_out_proj_kernel, n_in=len(xs)),
        name="out_proj",
        grid=(t // TM_PROJ,),
        in_specs=[pl.BlockSpec((TM_PROJ, x.shape[1]), lambda i: (i, 0)) for x in xs] + [
            pl.BlockSpec((None,) + w.shape[1:], lambda i: (layer, 0, 0), pipeline_mode=pl.Buffered(1)),
            pl.BlockSpec((1, D_MODEL), lambda i: (0, 0)),
            pl.BlockSpec((TM_PROJ, D_MODEL), lambda i: (i, 0)),
        ],
        out_specs=pl.BlockSpec((TM_PROJ, D_MODEL), lambda i: (i, 0)),
        out_shape=jax.ShapeDtypeStruct((t, D_MODEL), F32),
        compiler_params=_params("parallel"),
    )(*xs, w, g, h)


def _ple_kernel(h_ref, gpre_ref, wg_ref, p_ref, wp_ref, gpost_ref, o_ref):
    h = h_ref[...]
    gate = jax.nn.sigmoid(_dot(_rms(h, gpre_ref[...]).astype(BF16), wg_ref[...]))
    emb = _dot(p_ref[...].astype(BF16), wp_ref[...])
    o_ref[...] = h + _rms(gate * emb, gpost_ref[...])


def _ple(h, g_pre, w_gate, p, w_proj, g_post, layer):
    t = h.shape[0]
    return pl.pallas_call(
        _ple_kernel,
        name="gated_embed",
        grid=(t // TM_PROJ,),
        in_specs=[
            pl.BlockSpec((TM_PROJ, D_MODEL), lambda i: (i, 0)),
            pl.BlockSpec((1, D_MODEL), lambda i: (0, 0)),
            pl.BlockSpec((None,) + w_gate.shape[1:], lambda i: (layer, 0, 0), pipeline_mode=pl.Buffered(1)),
            pl.BlockSpec((None, TM_PROJ, PLE_DIM), lambda i: (layer, i, 0)),
            pl.BlockSpec((None,) + w_proj.shape[1:], lambda i: (layer, 0, 0), pipeline_mode=pl.Buffered(1)),
            pl.BlockSpec((1, D_MODEL), lambda i: (0, 0)),
        ],
        out_specs=pl.BlockSpec((TM_PROJ, D_MODEL), lambda i: (i, 0)),
        out_shape=jax.ShapeDtypeStruct((t, D_MODEL), F32),
        compiler_params=_params("parallel"),
    )(h, g_pre, w_gate, p, w_proj, g_post)


def _rope_tables(seq, rot_dim):
    inv = ROPE_THETA ** (-jnp.arange(0, rot_dim, 2, dtype=F32) / rot_dim)
    ang = jnp.arange(seq, dtype=F32)[:, None] * inv[None, :]
    return jnp.cos(ang), jnp.sin(ang)


def _lane_tables(cos, sin, tail):
    seq, half = cos.shape
    rest = LANES - 2 * half
    zeros = lambda n: jnp.zeros((seq, n), F32)
    cos_t = jnp.concatenate([cos, cos, jnp.full((seq, rest), tail, F32)], axis=1)
    sin_up = jnp.concatenate([zeros(half), sin, zeros(rest)], axis=1)
    sin_dn = jnp.concatenate([-sin, zeros(LANES - half)], axis=1)
    return cos_t, sin_up, sin_dn


def _row(v):
    return v.reshape(1, -1)


def kernel(x, p, ffn1_g_pre, ffn1_w_in, ffn1_w_out, ffn1_g_post, mix_g_pre, mix_g_post, ab_w_in, a_rel_bias, b_g_q, b_w_qup, b_g_kv, b_w_kvup, ab_w_out, c_w_in, c_lq1, c_lk1, c_lq2, c_lk2, c_g_sub, c_w_out, ffn2_g_pre, ffn2_w_in, ffn2_w_out, ffn2_g_post, ple_g_pre, ple_w_gate, ple_w_proj, ple_g_post):
    batch, seq, _ = x.shape
    t = batch * seq
    assert seq % TQ == 0 and seq % TM_IN == 0 and seq % TM_PROJ == 0 and t % TM_FFN == 0

    tab_b = _lane_tables(*_rope_tables(seq, B_ROPE_DIM), tail=0.0)
    tab_c = _lane_tables(*_rope_tables(seq, C_ROT_DIM), tail=1.0)

    (ple_w_gate, ple_w_proj, ab_w_in, ab_w_out, c_w_in, c_w_out, b_w_qup, b_w_kvup) = (
        w.astype(BF16) for w in (ple_w_gate, ple_w_proj, ab_w_in, ab_w_out, c_w_in, c_w_out,
                                 b_w_qup, b_w_kvup))
    ffn_f32 = [(w_in, w_out, i) for i in range(DEPTH)
               for w_in, w_out in ((ffn1_w_in, ffn1_w_out), (ffn2_w_in, ffn2_w_out))]
    ffn_f32.append(None)
    ffn_w = (ffn1_w_in[0].astype(BF16), ffn1_w_out[0].astype(BF16))
    pad_rope = LANES - B_ROPE_DIM
    w_b = jnp.pad(ab_w_in[:, :, 3 * A_WIDTH:], ((0, 0), (0, 0), (0, pad_rope)))
    w_q = b_w_qup.reshape(-1, B_Q_LORA, HEADS, B_NOPE_DIM + B_ROPE_DIM)
    w_q = jnp.pad(w_q, ((0, 0), (0, 0), (0, 0), (0, pad_rope))).reshape(-1, B_Q_LORA, HEADS * 2 * LANES)

    plain = (False,) * 8
    c_groups = (True,) * 4 + (False,) * 2

    h = x.reshape(t, D_MODEL)
    p = p.reshape(DEPTH, t, PLE_DIM)
    for i in range(DEPTH):
        j = i // 2
        h, *ffn_w = _ffn(h, _row(ffn1_g_pre[i]), *ffn_w, _row(ffn1_g_post[i]), ffn_f32[2 * i + 1])
        g_mix = _row(mix_g_pre[i])
        if i % 2 == 0:
            u_a = _proj(h, g_mix, ab_w_in, j, 3 * A_WIDTH, tab_b, seq, plain, 0)
            q_b, k_b, v_b = _mla_proj(h, g_mix, w_b[j], _row(b_g_q[j]), w_q[j], _row(b_g_kv[j]),
                                      b_w_kvup[j], tab_b, seq)
            o_a = _band_attn(u_a, a_rel_bias[j], batch, seq)
            o_b = _mla_attn(q_b, k_b, v_b, batch, seq)
            h = _out_proj([o_a, o_b], ab_w_out, _row(mix_g_post[i]), h, j)
        else:
            lambda_init = 0.8 - 0.6 * math.exp(-0.3 * i)
            u_c = _proj(h, g_mix, c_w_in, j, HEADS * C_HEAD_IN, tab_c, seq, c_groups, C_ROT_DIM // 2)
            o_c = _diff_attn(u_c, _row(c_lq1[j]), _row(c_lk1[j]), _row(c_lq2[j]), _row(c_lk2[j]),
                             _row(c_g_sub[j]), lambda_init, batch, seq)
            h = _out_proj([o_c], c_w_out, _row(mix_g_post[i]), h, j)
        h, *ffn_w = _ffn(h, _row(ffn2_g_pre[i]), *ffn_w, _row(ffn2_g_post[i]), ffn_f32[2 * i + 2])
        h = _ple(h, _row(ple_g_pre[i]), ple_w_gate, p, ple_w_proj, _row(ple_g_post[i]), i)
    return h.reshape(batch, seq, D_MODEL)
```

```python
import functools
import math

import jax
import jax.numpy as jnp
from jax import lax
from jax.experimental import pallas as pl
from jax.experimental.pallas import tpu as pltpu

D_MODEL = 2048
DEPTH = 4
CHUNK = 64
N_PREV_CHUNKS = 8
ROPE_THETA = 500000.0
PLE_DIM = 256
D_FF = 5632
NORM_EPS = 1e-6
NEG_INF = -1e30
LOG2E = math.log2(math.e)

HEADS = 8
HEAD_DIM = 128
A_WIDTH = HEADS * HEAD_DIM
REL_FUTURE = CHUNK - 1
REL_PAST_CLIP = 128
REL_TABLE = REL_FUTURE + REL_PAST_CLIP + 1

B_NOPE_DIM = 128
B_ROPE_DIM = 64
B_Q_LORA = 512
B_KV_LORA = 256

C_V_DIM = 2 * HEAD_DIM
C_ROT_DIM = HEAD_DIM // 4
C_HEAD_IN = 4 * HEAD_DIM + C_V_DIM

LANES = 128
VMEM_LIMIT = 58 * 1024 * 1024

FFN_VMEM_LIMIT = 62 * 1024 * 1024

TM_FFN = 1024
TF_FFN = 512
FFN_ROW_CHUNKS = 4
TM_IN = 1024
IN_ROW_CHUNKS = 4
PROJ_ROW_CHUNKS = 2
TM_PROJ = 512
TQ = 256
A_WIN = N_PREV_CHUNKS * CHUNK + TQ
BIAS_ROW = 1024

BF16 = jnp.bfloat16
F32 = jnp.float32


def _rms(x, g):
    return x * lax.rsqrt(jnp.mean(x * x, axis=-1, keepdims=True) + NORM_EPS) * g


def _dot(a, b):
    return jnp.dot(a, b, preferred_element_type=F32)


def _dot_t(a, b):
    return lax.dot_general(a, b, (((1,), (1,)), ((), ())), preferred_element_type=F32)


def _rope_group(u, cos, sin_up, sin_dn, half):
    return (u * cos + pltpu.roll(u, half, 1) * sin_up
            + pltpu.roll(u, LANES - half, 1) * sin_dn)


def _row_chunks(rows, n):
    return [slice(r, r + rows // n) for r in range(0, rows, rows // n)]


def _pipelined(chunks, pre, mm, epi):
    n = len(chunks)
    for rows in chunks[:2]:
        pre(rows)
    nxt = mm(chunks[0])
    for c, rows in enumerate(chunks):
        cur = nxt
        if c + 2 < n:
            pre(chunks[c + 2])
        if c + 1 < n:
            nxt = mm(chunks[c + 1])
        epi(rows, cur)


def _params(*sem):
    return pltpu.CompilerParams(dimension_semantics=sem, vmem_limit_bytes=VMEM_LIMIT)


def _ffn_kernel(*refs, cast_next):
    if cast_next:
        (h_ref, gpre_ref, wa_ref, wb_ref, wo_ref, gpost_ref, nin_ref, nout_ref,
         o_ref, nin_bf_ref, nout_bf_ref, hn_ref) = refs
        nin_bf_ref[...] = nin_ref[...].astype(BF16)
        nout_bf_ref[...] = nout_ref[...].astype(BF16)
    else:
        h_ref, gpre_ref, wa_ref, wb_ref, wo_ref, gpost_ref, o_ref, hn_ref = refs
    j = pl.program_id(1)
    last = pl.num_programs(1) - 1
    chunks = _row_chunks(TM_FFN, FFN_ROW_CHUNKS)

    def swiglu(rows):
        hn = hn_ref[rows, :]
        a = _dot(hn, wa_ref[...])
        b = _dot(hn, wb_ref[...])
        return _dot((a * jax.nn.sigmoid(a) * b).astype(BF16), wo_ref[...])

    def pre_norm(rows):
        hn_ref[rows, :] = _rms(h_ref[rows, :], gpre_ref[...]).astype(BF16)

    def first(rows, part):
        o_ref[rows, :] = part

    def finish(rows, part):
        o_ref[rows, :] = h_ref[rows, :] + 0.5 * _rms(o_ref[rows, :] + part, gpost_ref[...])

    @pl.when(j == 0)
    def _():
        _pipelined(chunks, pre_norm, swiglu, first)

    @pl.when((j > 0) & (j < last))
    def _():
        o_ref[...] += swiglu(slice(None))

    @pl.when(j == last)
    def _():
        _pipelined(chunks, lambda rows: None, swiglu, finish)


def _ffn(h, g_pre, w_in, w_out, g_post, cast_next=None):
    t = h.shape[0]
    ni, nf = t // TM_FFN, D_FF // TF_FFN
    assert nf >= 3
    in_specs = [
        pl.BlockSpec((TM_FFN, D_MODEL), lambda i, j: (i, 0)),
        pl.BlockSpec((1, D_MODEL), lambda i, j: (0, 0)),
        pl.BlockSpec((D_MODEL, TF_FFN), lambda i, j: (0, j)),
        pl.BlockSpec((D_MODEL, TF_FFN), lambda i, j: (0, j + nf)),
        pl.BlockSpec((TF_FFN, D_MODEL), lambda i, j: (j, 0)),
        pl.BlockSpec((1, D_MODEL), lambda i, j: (0, 0)),
    ]
    out_specs = [pl.BlockSpec((TM_FFN, D_MODEL), lambda i, j: (i, 0))]
    out_shape = [jax.ShapeDtypeStruct((t, D_MODEL), F32)]
    args = [h, g_pre, w_in, w_in, w_out, g_post]
    if cast_next is not None:
        nxt_in, nxt_out, layer = cast_next
        in_tile = (D_MODEL // ni, 2 * D_FF // nf)
        out_tile = (D_FF // nf, D_MODEL // ni)
        in_specs += [pl.BlockSpec((None,) + in_tile, lambda i, j: (layer, i, j)),
                     pl.BlockSpec((None,) + out_tile, lambda i, j: (layer, j, i))]
        out_specs += [pl.BlockSpec(in_tile, lambda i, j: (i, j)),
                      pl.BlockSpec(out_tile, lambda i, j: (j, i))]
        out_shape += [jax.ShapeDtypeStruct(nxt_in.shape[1:], BF16),
                      jax.ShapeDtypeStruct(nxt_out.shape[1:], BF16)]
        args += [nxt_in, nxt_out]
    return pl.pallas_call(
        functools.partial(_ffn_kernel, cast_next=cast_next is not None),
        name="ffn",
        grid=(ni, nf),
        in_specs=in_specs,
        out_specs=out_specs,
        out_shape=out_shape,
        scratch_shapes=[pltpu.VMEM((TM_FFN, D_MODEL), BF16)],
        compiler_params=pltpu.CompilerParams(dimension_semantics=("parallel", "arbitrary"),
                                             vmem_limit_bytes=FFN_VMEM_LIMIT),
    )(*args)


def _proj_kernel(h_ref, g_ref, w_ref, cos_ref, sup_ref, sdn_ref, o_ref, hn_ref, *, rope_groups, half):
    chunks = _row_chunks(TM_IN, IN_ROW_CHUNKS)

    def pre_norm(rows):
        hn_ref[rows, :] = _rms(h_ref[rows, :], g_ref[...]).astype(BF16)

    def project(rows):
        return _dot(hn_ref[rows, :], w_ref[...])

    def emit(rows, u):
        if not any(rope_groups):
            o_ref[rows, :] = u.astype(o_ref.dtype)
            return
        cos, sup, sdn = cos_ref[rows, :], sup_ref[rows, :], sdn_ref[rows, :]
        for c, rotate in enumerate(rope_groups):
            sl = slice(c * LANES, (c + 1) * LANES)
            x = _rope_group(u[:, sl], cos, sup, sdn, half) if rotate else u[:, sl]
            o_ref[rows, sl] = x.astype(o_ref.dtype)

    @pl.when(pl.program_id(1) == 0)
    def _():
        _pipelined(chunks, pre_norm, project, emit)

    @pl.when(pl.program_id(1) > 0)
    def _():
        _pipelined(chunks, lambda rows: None, project, emit)


def _proj(h, g, w, layer, n_cols, tables, seq, rope_groups, half):
    t = h.shape[0]
    tn = len(rope_groups) * LANES
    blocks_per_seq = seq // TM_IN
    tab_spec = pl.BlockSpec((TM_IN, LANES), lambda i, j: (i % blocks_per_seq, 0))
    return pl.pallas_call(
        functools.partial(_proj_kernel, rope_groups=rope_groups, half=half),
        name="in_proj",
        grid=(t // TM_IN, n_cols // tn),
        in_specs=[
            pl.BlockSpec((TM_IN, D_MODEL), lambda i, j: (i, 0)),
            pl.BlockSpec((1, D_MODEL), lambda i, j: (0, 0)),
            pl.BlockSpec((None, D_MODEL, tn), lambda i, j: (layer, 0, j)),
            tab_spec, tab_spec, tab_spec,
        ],
        out_specs=pl.BlockSpec((TM_IN, tn), lambda i, j: (i, j)),
        out_shape=jax.ShapeDtypeStruct((t, n_cols), BF16),
        scratch_shapes=[pltpu.VMEM((TM_IN, D_MODEL), BF16)],
        compiler_params=_params("parallel", "arbitrary"),
    )(h, g, w, *tables)


def _mla_proj_kernel(h_ref, g_ref, wb_ref, gq_ref, wq_ref, gkv_ref, wkv_ref,
                     cos_ref, sup_ref, sdn_ref, q_ref, k_ref, v_ref):
    hn = _rms(h_ref[...], g_ref[...]).astype(BF16)
    u = _dot(hn, wb_ref[...])
    cq = u[:, :B_Q_LORA]
    ckv = u[:, B_Q_LORA:B_Q_LORA + B_KV_LORA]
    kr = u[:, B_Q_LORA + B_KV_LORA:]
    cos, sup, sdn = cos_ref[...], sup_ref[...], sdn_ref[...]
    half = B_ROPE_DIM // 2

    q = _dot(_rms(cq, gq_ref[...]).astype(BF16), wq_ref[...])
    kv = _dot(_rms(ckv, gkv_ref[...]).astype(BF16), wkv_ref[...])
    k_rope = _rope_group(kr, cos, sup, sdn, half).astype(BF16)
    for hd in range(HEADS):
        lo, mid, hi = 2 * hd * LANES, (2 * hd + 1) * LANES, (2 * hd + 2) * LANES
        q_ref[:, lo:mid] = q[:, lo:mid].astype(BF16)
        q_ref[:, mid:hi] = _rope_group(q[:, mid:hi], cos, sup, sdn, half).astype(BF16)
        k_ref[:, lo:mid] = kv[:, lo:mid].astype(BF16)
        k_ref[:, mid:hi] = k_rope
        v_ref[:, hd * LANES:(hd + 1) * LANES] = kv[:, mid:hi].astype(BF16)


def _mla_proj(h, g, wb, gq, wq, gkv, wkv, tables, seq):
    t = h.shape[0]
    blocks_per_seq = seq // TM_PROJ
    tab_spec = pl.BlockSpec((TM_PROJ, LANES), lambda i: (i % blocks_per_seq, 0))
    full = lambda a: pl.BlockSpec(a.shape, lambda i: (0, 0))
    wide = HEADS * 2 * LANES
    return pl.pallas_call(
        _mla_proj_kernel,
        name="mla_proj",
        grid=(t // TM_PROJ,),
        in_specs=[pl.BlockSpec((TM_PROJ, D_MODEL), lambda i: (i, 0)),
                  full(g), full(wb), full(gq), full(wq), full(gkv), full(wkv),
                  tab_spec, tab_spec, tab_spec],
        out_specs=[pl.BlockSpec((TM_PROJ, wide), lambda i: (i, 0)),
                   pl.BlockSpec((TM_PROJ, wide), lambda i: (i, 0)),
                   pl.BlockSpec((TM_PROJ, A_WIDTH), lambda i: (i, 0))],
        out_shape=[jax.ShapeDtypeStruct((t, wide), BF16),
                   jax.ShapeDtypeStruct((t, wide), BF16),
                   jax.ShapeDtypeStruct((t, A_WIDTH), BF16)],
        compiler_params=_params("parallel"),
    )(h, g, wb, gq, wq, gkv, wkv, *tables)


def _chunk_of(pos):
    return jnp.right_shift(pos, CHUNK.bit_length() - 1)


def _chunk_causal_mask():
    qc = _chunk_of(lax.broadcasted_iota(jnp.int32, (TQ, TQ), 0))
    kc = _chunk_of(lax.broadcasted_iota(jnp.int32, (TQ, TQ), 1))
    return kc <= qc


def _causal_weights(s, qs, mask, c):
    s_diag = jnp.where(mask, s[:, qs:], NEG_INF)
    m = jnp.max(s_diag, axis=-1, keepdims=True)
    if qs == 0:
        return [jnp.exp2((s_diag - m) * c)]
    s_off = s[:, :qs]
    m = jnp.maximum(m, jnp.max(s_off, axis=-1, keepdims=True))
    return [jnp.exp2((s_off - m) * c), jnp.exp2((s_diag - m) * c)]


def _as_bf16_row(parts):
    parts = [x.astype(BF16) for x in parts]
    return parts[0] if len(parts) == 1 else jnp.concatenate(parts, axis=1)


def _fill_value_and_ones(vaug_ref, v_ref):
    vaug_ref[:, :LANES] = v_ref[...]
    vaug_ref[:, LANES:] = jnp.ones((v_ref.shape[0], LANES), BF16)


def _mla_attn_kernel(q_ref, k_ref, v_ref, o_ref, vaug_ref, *, seq, scale):
    mask = _chunk_causal_mask()
    _fill_value_and_ones(vaug_ref, v_ref)
    scores = lambda qs: _dot_t(q_ref[qs:qs + TQ, :], k_ref[0:qs + TQ, :])
    s_next = scores(0)
    for qs in range(0, seq, TQ):
        qe = qs + TQ
        s, s_next = s_next, (scores(qe) if qe < seq else None)
        p = _as_bf16_row(_causal_weights(s, qs, mask, scale * LOG2E))
        o = _dot(p, vaug_ref[0:qe, :])
        o_ref[qs:qe, :] = (o[:, :LANES] / o[:, LANES:]).astype(o_ref.dtype)


def _mla_attn(q, k, v, batch, seq):
    t = q.shape[0]
    scale = (B_NOPE_DIM + B_ROPE_DIM) ** -0.5
    return pl.pallas_call(
        functools.partial(_mla_attn_kernel, seq=seq, scale=scale),
        name="mla_attn",
        grid=(batch, HEADS),
        in_specs=[pl.BlockSpec((seq, 2 * LANES), lambda b, h: (b, h)),
                  pl.BlockSpec((seq, 2 * LANES), lambda b, h: (b, h)),
                  pl.BlockSpec((seq, LANES), lambda b, h: (b, h))],
        out_specs=pl.BlockSpec((seq, LANES), lambda b, h: (b, h)),
        out_shape=jax.ShapeDtypeStruct((t, A_WIDTH), BF16),
        scratch_shapes=[pltpu.VMEM((seq, 2 * LANES), BF16)],
        compiler_params=_params("parallel", "parallel"),
    )(q, k, v)


def _diff_attn_kernel(q1_ref, q2_ref, k1_ref, k2_ref, v_ref, lq1_ref, lk1_ref, lq2_ref, lk2_ref,
                      gsub_ref, o_ref, *, seq, scale, lambda_init):
    mask = _chunk_causal_mask()
    lam = (jnp.exp(jnp.sum(lq1_ref[...] * lk1_ref[...], axis=-1, keepdims=True))
           - jnp.exp(jnp.sum(lq2_ref[...] * lk2_ref[...], axis=-1, keepdims=True)) + lambda_init)
    c = scale * LOG2E
    scores = lambda qs: (_dot_t(q1_ref[qs:qs + TQ, :], k1_ref[0:qs + TQ, :]),
                         _dot_t(q2_ref[qs:qs + TQ, :], k2_ref[0:qs + TQ, :]))
    s_next = scores(0)
    for qs in range(0, seq, TQ):
        qe = qs + TQ
        (s1, s2), s_next = s_next, (scores(qe) if qe < seq else None)
        w1 = _causal_weights(s1, qs, mask, c)
        w2 = _causal_weights(s2, qs, mask, c)
        l1 = sum(jnp.sum(x, axis=-1, keepdims=True) for x in w1)
        l2 = sum(jnp.sum(x, axis=-1, keepdims=True) for x in w2)
        o12 = _dot(jnp.concatenate([_as_bf16_row(w1), _as_bf16_row(w2)], axis=0), v_ref[0:qe, :])
        o = o12[:TQ] * (1.0 / l1) - o12[TQ:] * (lam / l2)
        o_ref[qs:qe, :] = (_rms(o, gsub_ref[...]) * (1.0 - lambda_init)).astype(o_ref.dtype)


def _diff_attn(u, lq1, lk1, lq2, lk2, g_sub, lambda_init, batch, seq):
    t = u.shape[0]
    groups = C_HEAD_IN // LANES
    part = lambda c: pl.BlockSpec((seq, LANES), lambda b, h: (b, groups * h + c))
    vec = pl.BlockSpec((1, LANES), lambda b, h: (0, 0))
    return pl.pallas_call(
        functools.partial(_diff_attn_kernel, seq=seq, scale=HEAD_DIM ** -0.5, lambda_init=lambda_init),
        name="diff_attn",
        grid=(batch, HEADS),
        in_specs=[part(0), part(1), part(2), part(3),
                  pl.BlockSpec((seq, C_V_DIM), lambda b, h: (b, (C_HEAD_IN // C_V_DIM) * h + 2)),
                  vec, vec, vec, vec,
                  pl.BlockSpec((1, C_V_DIM), lambda b, h: (0, 0))],
        out_specs=pl.BlockSpec((seq, C_V_DIM), lambda b, h: (b, h)),
        out_shape=jax.ShapeDtypeStruct((t, HEADS * C_V_DIM), BF16),
        compiler_params=_params("parallel", "parallel"),
    )(u, u, u, u, u, lq1, lk1, lq2, lk2, g_sub)


def _band_bias_kernel(tab_ref, o_ref, *, inv_scale):
    hd = pl.program_id(0)
    first = N_PREV_CHUNKS * CHUNK
    kj = lax.broadcasted_iota(jnp.int32, (8, BIAS_ROW), 1)
    dist = jnp.where(kj < A_WIN, first - kj, first + BIAS_ROW - kj)
    idx = jnp.clip(dist, -REL_FUTURE, REL_PAST_CLIP) + REL_FUTURE

    def body(tt, acc):
        return jnp.where(idx == tt, tab_ref[hd, tt], acc)

    base = lax.fori_loop(0, REL_TABLE, body, jnp.zeros((8, BIAS_ROW), F32))
    sub = lax.broadcasted_iota(jnp.int32, (8, BIAS_ROW), 0)
    for bit in range(3):
        rotated = pltpu.roll(base, 1 << bit, 1)
        base = jnp.where((jnp.right_shift(sub, bit) & 1) == 1, rotated, base)
    bias = jnp.concatenate([pltpu.roll(base, 8 * b, 1) if b else base for b in range(TQ // 8)], axis=0)
    qc = _chunk_of(lax.broadcasted_iota(jnp.int32, (TQ, A_WIN), 0))
    kc = _chunk_of(lax.broadcasted_iota(jnp.int32, (TQ, A_WIN), 1))
    valid = (kc >= qc) & (kc <= qc + N_PREV_CHUNKS)
    o_ref[0] = jnp.where(valid, bias[:, :A_WIN] * inv_scale, NEG_INF)


def _band_bias(rel_table, scale):
    assert BIAS_ROW >= A_WIN + TQ
    return pl.pallas_call(
        functools.partial(_band_bias_kernel, inv_scale=1.0 / scale),
        name="band_bias",
        grid=(HEADS,),
        in_specs=[pl.BlockSpec(memory_space=pltpu.SMEM)],
        out_specs=pl.BlockSpec((1, TQ, A_WIN), lambda h: (h, 0, 0)),
        out_shape=jax.ShapeDtypeStruct((HEADS, TQ, A_WIN), F32),
        compiler_params=_params("parallel"),
    )(rel_table)


def _band_attn_kernel(q_ref, k_ref, v_ref, bias_ref, o_ref, vaug_ref, *, seq, scale):
    _fill_value_and_ones(vaug_ref, v_ref)
    window = lambda qs: max(0, qs - N_PREV_CHUNKS * CHUNK)
    scores = lambda qs: _dot_t(q_ref[qs:qs + TQ, :], k_ref[window(qs):qs + TQ, :])
    s_next = scores(0)
    for qs in range(0, seq, TQ):
        qe, ws = qs + TQ, window(qs)
        s, s_next = s_next, (scores(qe) if qe < seq else None)
        s = s + bias_ref[0, :, A_WIN - (qe - ws):]
        p = jnp.exp2((s - jnp.max(s, axis=-1, keepdims=True)) * (scale * LOG2E))
        o = _dot(p.astype(BF16), vaug_ref[ws:qe, :])
        o_ref[qs:qe, :] = (o[:, :LANES] / o[:, LANES:]).astype(o_ref.dtype)


def _band_attn(u, rel_table, batch, seq):
    t = u.shape[0]
    scale = HEAD_DIM ** -0.5
    part = lambda c: pl.BlockSpec((seq, LANES), lambda b, h: (b, c * HEADS + h))
    return pl.pallas_call(
        functools.partial(_band_attn_kernel, seq=seq, scale=scale),
        name="band_attn",
        grid=(batch, HEADS),
        in_specs=[part(0), part(1), part(2),
                  pl.BlockSpec((1, TQ, A_WIN), lambda b, h: (h, 0, 0))],
        out_specs=pl.BlockSpec((seq, LANES), lambda b, h: (b, h)),
        out_shape=jax.ShapeDtypeStruct((t, A_WIDTH), BF16),
        scratch_shapes=[pltpu.VMEM((seq, 2 * LANES), BF16)],
        compiler_params=_params("parallel", "parallel"),
    )(u, u, u, _band_bias(rel_table, scale))


def _out_proj_kernel(*refs, n_in):
    x_refs, (w_ref, g_ref, h_ref, o_ref) = refs[:n_in], refs[n_in:]

    def project(rows):
        mix = None
        k0 = 0
        for x_ref in x_refs:
            k1 = k0 + x_ref.shape[1]
            part = _dot(x_ref[rows, :], w_ref[k0:k1, :])
            mix = part if mix is None else mix + part
            k0 = k1
        return mix

    def finish(rows, mix):
        o_ref[rows, :] = h_ref[rows, :] + _rms(mix, g_ref[...])

    _pipelined(_row_chunks(TM_PROJ, PROJ_ROW_CHUNKS), lambda rows: None, project, finish)


def _out_proj(xs, w, g, h, layer):
    t = h.shape[0]
    return pl.pallas_call(
        functools.partial(_out_proj_kernel, n_in=len(xs)),
        name="out_proj",
        grid=(t // TM_PROJ,),
        in_specs=[pl.BlockSpec((TM_PROJ, x.shape[1]), lambda i: (i, 0)) for x in xs] + [
            pl.BlockSpec((None,) + w.shape[1:], lambda i: (layer, 0, 0), pipeline_mode=pl.Buffered(1)),
            pl.BlockSpec((1, D_MODEL), lambda i: (0, 0)),
            pl.BlockSpec((TM_PROJ, D_MODEL), lambda i: (i, 0)),
        ],
        out_specs=pl.BlockSpec((TM_PROJ, D_MODEL), lambda i: (i, 0)),
        out_shape=jax.ShapeDtypeStruct((t, D_MODEL), F32),
        compiler_params=_params("parallel"),
    )(*xs, w, g, h)


def _ple_kernel(h_ref, gpre_ref, wg_ref, p_ref, wp_ref, gpost_ref, o_ref, hn_ref):
    def pre_norm(rows):
        hn_ref[rows, :] = _rms(h_ref[rows, :], gpre_ref[...]).astype(BF16)

    def project(rows):
        return _dot(hn_ref[rows, :], wg_ref[...]), _dot(p_ref[rows, :].astype(BF16), wp_ref[...])

    def finish(rows, gate_emb):
        gate, emb = gate_emb
        o_ref[rows, :] = h_ref[rows, :] + _rms(jax.nn.sigmoid(gate) * emb, gpost_ref[...])

    _pipelined(_row_chunks(TM_PROJ, PROJ_ROW_CHUNKS), pre_norm, project, finish)


def _ple(h, g_pre, w_gate, p, w_proj, g_post, layer):
    t = h.shape[0]
    return pl.pallas_call(
        _ple_kernel,
        name="gated_embed",
        grid=(t // TM_PROJ,),
        in_specs=[
            pl.BlockSpec((TM_PROJ, D_MODEL), lambda i: (i, 0)),
            pl.BlockSpec((1, D_MODEL), lambda i: (0, 0)),
            pl.BlockSpec((None,) + w_gate.shape[1:], lambda i: (layer, 0, 0), pipeline_mode=pl.Buffered(1)),
            pl.BlockSpec((None, TM_PROJ, PLE_DIM), lambda i: (layer, i, 0)),
            pl.BlockSpec((None,) + w_proj.shape[1:], lambda i: (layer, 0, 0), pipeline_mode=pl.Buffered(1)),
            pl.BlockSpec((1, D_MODEL), lambda i: (0, 0)),
        ],
        out_specs=pl.BlockSpec((TM_PROJ, D_MODEL), lambda i: (i, 0)),
        out_shape=jax.ShapeDtypeStruct((t, D_MODEL), F32),
        scratch_shapes=[pltpu.VMEM((TM_PROJ, D_MODEL), BF16)],
        compiler_params=_params("parallel"),
    )(h, g_pre, w_gate, p, w_proj, g_post)


def _rope_tables(seq, rot_dim):
    inv = ROPE_THETA ** (-jnp.arange(0, rot_dim, 2, dtype=F32) / rot_dim)
    ang = jnp.arange(seq, dtype=F32)[:, None] * inv[None, :]
    return jnp.cos(ang), jnp.sin(ang)


def _lane_tables(cos, sin, tail):
    seq, half = cos.shape
    rest = LANES - 2 * half
    zeros = lambda n: jnp.zeros((seq, n), F32)
    cos_t = jnp.concatenate([cos, cos, jnp.full((seq, rest), tail, F32)], axis=1)
    sin_up = jnp.concatenate([zeros(half), sin, zeros(rest)], axis=1)
    sin_dn = jnp.concatenate([-sin, zeros(LANES - half)], axis=1)
    return cos_t, sin_up, sin_dn


def _row(v):
    return v.reshape(1, -1)


def kernel(x, p, ffn1_g_pre, ffn1_w_in, ffn1_w_out, ffn1_g_post, mix_g_pre, mix_g_post, ab_w_in, a_rel_bias, b_g_q, b_w_qup, b_g_kv, b_w_kvup, ab_w_out, c_w_in, c_lq1, c_lk1, c_lq2, c_lk2, c_g_sub, c_w_out, ffn2_g_pre, ffn2_w_in, ffn2_w_out, ffn2_g_post, ple_g_pre, ple_w_gate, ple_w_proj, ple_g_post):
    batch, seq, _ = x.shape
    t = batch * seq
    assert seq % TQ == 0 and seq % TM_IN == 0 and seq % TM_PROJ == 0 and t % TM_FFN == 0

    tab_b = _lane_tables(*_rope_tables(seq, B_ROPE_DIM), tail=0.0)
    tab_c = _lane_tables(*_rope_tables(seq, C_ROT_DIM), tail=1.0)

    (ple_w_gate, ple_w_proj, ab_w_in, ab_w_out, c_w_in, c_w_out, b_w_qup, b_w_kvup) = (
        w.astype(BF16) for w in (ple_w_gate, ple_w_proj, ab_w_in, ab_w_out, c_w_in, c_w_out,
                                 b_w_qup, b_w_kvup))
    ffn_f32 = [(w_in, w_out, i) for i in range(DEPTH)
               for w_in, w_out in ((ffn1_w_in, ffn1_w_out), (ffn2_w_in, ffn2_w_out))]
    ffn_f32.append(None)
    ffn_w = (ffn1_w_in[0].astype(BF16), ffn1_w_out[0].astype(BF16))
    pad_rope = LANES - B_ROPE_DIM
    w_b = jnp.pad(ab_w_in[:, :, 3 * A_WIDTH:], ((0, 0), (0, 0), (0, pad_rope)))
    w_q = b_w_qup.reshape(-1, B_Q_LORA, HEADS, B_NOPE_DIM + B_ROPE_DIM)
    w_q = jnp.pad(w_q, ((0, 0), (0, 0), (0, 0), (0, pad_rope))).reshape(-1, B_Q_LORA, HEADS * 2 * LANES)

    plain = (False,) * 8
    c_groups = (True,) * 4 + (False,) * 2

    h = x.reshape(t, D_MODEL)
    p = p.reshape(DEPTH, t, PLE_DIM)
    for i in range(DEPTH):
        j = i // 2
        h, *ffn_w = _ffn(h, _row(ffn1_g_pre[i]), *ffn_w, _row(ffn1_g_post[i]), ffn_f32[2 * i + 1])
        g_mix = _row(mix_g_pre[i])
        if i % 2 == 0:
            u_a = _proj(h, g_mix, ab_w_in, j, 3 * A_WIDTH, tab_b, seq, plain, 0)
            q_b, k_b, v_b = _mla_proj(h, g_mix, w_b[j], _row(b_g_q[j]), w_q[j], _row(b_g_kv[j]),
                                      b_w_kvup[j], tab_b, seq)
            o_a = _band_attn(u_a, a_rel_bias[j], batch, seq)
            o_b = _mla_attn(q_b, k_b, v_b, batch, seq)
            h = _out_proj([o_a, o_b], ab_w_out, _row(mix_g_post[i]), h, j)
        else:
            lambda_init = 0.8 - 0.6 * math.exp(-0.3 * i)
            u_c = _proj(h, g_mix, c_w_in, j, HEADS * C_HEAD_IN, tab_c, seq, c_groups, C_ROT_DIM // 2)
            o_c = _diff_attn(u_c, _row(c_lq1[j]), _row(c_lk1[j]), _row(c_lq2[j]), _row(c_lk2[j]),
                             _row(c_g_sub[j]), lambda_init, batch, seq)
            h = _out_proj([o_c], c_w_out, _row(mix_g_post[i]), h, j)
        h, *ffn_w = _ffn(h, _row(ffn2_g_pre[i]), *ffn_w, _row(ffn2_g_post[i]), ffn_f32[2 * i + 2])
        h = _ple(h, _row(ple_g_pre[i]), ple_w_gate, p, ple_w_proj, _row(ple_g_post[i]), i)
    return h.reshape(batch, seq, D_MODEL)
```

```python
import functools
import math

import jax
import jax.numpy as jnp
from jax import lax
from jax.experimental import pallas as pl
from jax.experimental.pallas import tpu as pltpu

D_MODEL = 2048
DEPTH = 4
CHUNK = 64
N_PREV_CHUNKS = 8
ROPE_THETA = 500000.0
PLE_DIM = 256
D_FF = 5632
NORM_EPS = 1e-6
NEG_INF = -1e30
LOG2E = math.log2(math.e)

HEADS = 8
HEAD_DIM = 128
A_WIDTH = HEADS * HEAD_DIM
REL_FUTURE = CHUNK - 1
REL_PAST_CLIP = 128
REL_TABLE = REL_FUTURE + REL_PAST_CLIP + 1

B_NOPE_DIM = 128
B_ROPE_DIM = 64
B_Q_LORA = 512
B_KV_LORA = 256

C_V_DIM = 2 * HEAD_DIM
C_ROT_DIM = HEAD_DIM // 4
C_HEAD_IN = 4 * HEAD_DIM + C_V_DIM

LANES = 128
VMEM_LIMIT = 60 * 1024 * 1024

FFN_VMEM_LIMIT = 62 * 1024 * 1024

TM_FFN = 1024
TF_FFN = 512
FFN_ROW_CHUNKS = 4
TM_IN = 1024
IN_ROW_CHUNKS = 4
TM_PROJ = 512
PROJ_ROW_CHUNKS = 2
TM_MLA = 512
TQ = 256
A_WIN = N_PREV_CHUNKS * CHUNK + TQ
BIAS_ROW = 1024

BF16 = jnp.bfloat16
F32 = jnp.float32


def _rms(x, g):
    return x * lax.rsqrt(jnp.mean(x * x, axis=-1, keepdims=True) + NORM_EPS) * g


def _dot(a, b):
    return jnp.dot(a, b, preferred_element_type=F32)


def _dot_t(a, b):
    return lax.dot_general(a, b, (((1,), (1,)), ((), ())), preferred_element_type=F32)


def _rope_group(u, cos, sin_up, sin_dn, half):
    return (u * cos + pltpu.roll(u, half, 1) * sin_up
            + pltpu.roll(u, LANES - half, 1) * sin_dn)


def _row_chunks(rows, n):
    return [slice(r, r + rows // n) for r in range(0, rows, rows // n)]


def _pipelined(chunks, pre, mm, epi):
    n = len(chunks)
    for rows in chunks[:2]:
        pre(rows)
    nxt = mm(chunks[0])
    for c, rows in enumerate(chunks):
        cur = nxt
        if c + 2 < n:
            pre(chunks[c + 2])
        if c + 1 < n:
            nxt = mm(chunks[c + 1])
        epi(rows, cur)


def _params(*sem):
    return pltpu.CompilerParams(dimension_semantics=sem, vmem_limit_bytes=VMEM_LIMIT)


def _ffn_kernel(*refs, cast_next):
    if cast_next:
        (h_ref, gpre_ref, wa_ref, wb_ref, wo_ref, gpost_ref, nin_ref, nout_ref,
         o_ref, nin_bf_ref, nout_bf_ref, hn_ref) = refs
        nin_bf_ref[...] = nin_ref[...].astype(BF16)
        nout_bf_ref[...] = nout_ref[...].astype(BF16)
    else:
        h_ref, gpre_ref, wa_ref, wb_ref, wo_ref, gpost_ref, o_ref, hn_ref = refs
    j = pl.program_id(1)
    last = pl.num_programs(1) - 1
    chunks = _row_chunks(TM_FFN, FFN_ROW_CHUNKS)

    def swiglu(rows):
        hn = hn_ref[rows, :]
        a = _dot(hn, wa_ref[...])
        b = _dot(hn, wb_ref[...])
        return _dot((a * jax.nn.sigmoid(a) * b).astype(BF16), wo_ref[...])

    def pre_norm(rows):
        hn_ref[rows, :] = _rms(h_ref[rows, :], gpre_ref[...]).astype(BF16)

    def first(rows, part):
        o_ref[rows, :] = part

    def finish(rows, part):
        o_ref[rows, :] = h_ref[rows, :] + 0.5 * _rms(o_ref[rows, :] + part, gpost_ref[...])

    @pl.when(j == 0)
    def _():
        _pipelined(chunks, pre_norm, swiglu, first)

    @pl.when((j > 0) & (j < last))
    def _():
        o_ref[...] += swiglu(slice(None))

    @pl.when(j == last)
    def _():
        _pipelined(chunks, lambda rows: None, swiglu, finish)


def _ffn(h, g_pre, w_in, w_out, g_post, cast_next=None):
    t = h.shape[0]
    ni, nf = t // TM_FFN, D_FF // TF_FFN
    assert nf >= 3
    in_specs = [
        pl.BlockSpec((TM_FFN, D_MODEL), lambda i, j: (i, 0)),
        pl.BlockSpec((1, D_MODEL), lambda i, j: (0, 0)),
        pl.BlockSpec((D_MODEL, TF_FFN), lambda i, j: (0, j)),
        pl.BlockSpec((D_MODEL, TF_FFN), lambda i, j: (0, j + nf)),
        pl.BlockSpec((TF_FFN, D_MODEL), lambda i, j: (j, 0)),
        pl.BlockSpec((1, D_MODEL), lambda i, j: (0, 0)),
    ]
    out_specs = [pl.BlockSpec((TM_FFN, D_MODEL), lambda i, j: (i, 0))]
    out_shape = [jax.ShapeDtypeStruct((t, D_MODEL), F32)]
    args = [h, g_pre, w_in, w_in, w_out, g_post]
    if cast_next is not None:
        nxt_in, nxt_out, layer = cast_next
        in_tile = (D_MODEL // ni, 2 * D_FF // nf)
        out_tile = (D_FF // nf, D_MODEL // ni)
        in_specs += [pl.BlockSpec((None,) + in_tile, lambda i, j: (layer, i, j)),
                     pl.BlockSpec((None,) + out_tile, lambda i, j: (layer, j, i))]
        out_specs += [pl.BlockSpec(in_tile, lambda i, j: (i, j)),
                      pl.BlockSpec(out_tile, lambda i, j: (j, i))]
        out_shape += [jax.ShapeDtypeStruct(nxt_in.shape[1:], BF16),
                      jax.ShapeDtypeStruct(nxt_out.shape[1:], BF16)]
        args += [nxt_in, nxt_out]
    return pl.pallas_call(
        functools.partial(_ffn_kernel, cast_next=cast_next is not None),
        name="ffn",
        grid=(ni, nf),
        in_specs=in_specs,
        out_specs=out_specs,
        out_shape=out_shape,
        scratch_shapes=[pltpu.VMEM((TM_FFN, D_MODEL), BF16)],
        compiler_params=pltpu.CompilerParams(dimension_semantics=("parallel", "arbitrary"),
                                             vmem_limit_bytes=FFN_VMEM_LIMIT),
    )(*args)


def _proj_kernel(*refs, rope_groups, half, n_cast):
    h_ref, g_ref, w_ref, cos_ref, sup_ref, sdn_ref = refs[:6]
    cast_src, o_ref, cast_dst, hn_ref = refs[6:6 + n_cast], refs[6 + n_cast], refs[7 + n_cast:-1], refs[-1]
    chunks = _row_chunks(TM_IN, IN_ROW_CHUNKS)

    def pre_norm(rows):
        hn_ref[rows, :] = _rms(h_ref[rows, :], g_ref[...]).astype(BF16)

    def project(rows):
        return _dot(hn_ref[rows, :], w_ref[...])

    def emit(rows, u):
        if not any(rope_groups):
            o_ref[rows, :] = u.astype(o_ref.dtype)
            return
        cos, sup, sdn = cos_ref[rows, :], sup_ref[rows, :], sdn_ref[rows, :]
        for c, rotate in enumerate(rope_groups):
            sl = slice(c * LANES, (c + 1) * LANES)
            x = _rope_group(u[:, sl], cos, sup, sdn, half) if rotate else u[:, sl]
            o_ref[rows, sl] = x.astype(o_ref.dtype)

    @pl.when(pl.program_id(1) == 0)
    def _():
        for src, dst in zip(cast_src, cast_dst):
            dst[...] = src[...].astype(BF16)
        _pipelined(chunks, pre_norm, project, emit)

    @pl.when(pl.program_id(1) > 0)
    def _():
        _pipelined(chunks, lambda rows: None, project, emit)


def _proj(h, g, w, n_cols, tables, seq, rope_groups, half, casts=()):
    t = h.shape[0]
    ni = t // TM_IN
    tn = len(rope_groups) * LANES
    blocks_per_seq = seq // TM_IN
    tab_spec = pl.BlockSpec((TM_IN, LANES), lambda i, j: (i % blocks_per_seq, 0))
    cast_in, cast_out, cast_shape = [], [], []
    for src, layer in casts:
        rows, cols = src.shape[1:]
        cast_in.append(pl.BlockSpec((None, rows // ni, cols), lambda i, j, layer=layer: (layer, i, 0)))
        cast_out.append(pl.BlockSpec((rows // ni, cols), lambda i, j: (i, 0)))
        cast_shape.append(jax.ShapeDtypeStruct((rows, cols), BF16))
    return pl.pallas_call(
        functools.partial(_proj_kernel, rope_groups=rope_groups, half=half, n_cast=len(casts)),
        name="in_proj",
        grid=(ni, n_cols // tn),
        in_specs=[
            pl.BlockSpec((TM_IN, D_MODEL), lambda i, j: (i, 0)),
            pl.BlockSpec((1, D_MODEL), lambda i, j: (0, 0)),
            pl.BlockSpec((D_MODEL, tn), lambda i, j: (0, j)),
            tab_spec, tab_spec, tab_spec,
        ] + cast_in,
        out_specs=[pl.BlockSpec((TM_IN, tn), lambda i, j: (i, j))] + cast_out,
        out_shape=[jax.ShapeDtypeStruct((t, n_cols), BF16)] + cast_shape,
        scratch_shapes=[pltpu.VMEM((TM_IN, D_MODEL), BF16)],
        compiler_params=_params("parallel", "arbitrary"),
    )(h, g, w, *tables, *[src for src, _ in casts])


def _mla_proj_kernel(h_ref, g_ref, wb_ref, gq_ref, wq_ref, gkv_ref, wkv_ref,
                     cos_ref, sup_ref, sdn_ref, q_ref, k_ref, v_ref):
    hn = _rms(h_ref[...], g_ref[...]).astype(BF16)
    u = _dot(hn, wb_ref[...])
    cq = u[:, :B_Q_LORA]
    ckv = u[:, B_Q_LORA:B_Q_LORA + B_KV_LORA]
    kr = u[:, B_Q_LORA + B_KV_LORA:]
    cos, sup, sdn = cos_ref[...], sup_ref[...], sdn_ref[...]
    half = B_ROPE_DIM // 2

    q = _dot(_rms(cq, gq_ref[...]).astype(BF16), wq_ref[...])
    kv = _dot(_rms(ckv, gkv_ref[...]).astype(BF16), wkv_ref[...])
    k_rope = _rope_group(kr, cos, sup, sdn, half).astype(BF16)
    for hd in range(HEADS):
        lo, mid, hi = 2 * hd * LANES, (2 * hd + 1) * LANES, (2 * hd + 2) * LANES
        q_ref[:, lo:mid] = q[:, lo:mid].astype(BF16)
        q_ref[:, mid:hi] = _rope_group(q[:, mid:hi], cos, sup, sdn, half).astype(BF16)
        k_ref[:, lo:mid] = kv[:, lo:mid].astype(BF16)
        k_ref[:, mid:hi] = k_rope
        v_ref[:, hd * LANES:(hd + 1) * LANES] = kv[:, mid:hi].astype(BF16)


def _mla_proj(h, g, wb, gq, wq, gkv, wkv, tables, seq):
    t = h.shape[0]
    blocks_per_seq = seq // TM_MLA
    tab_spec = pl.BlockSpec((TM_MLA, LANES), lambda i: (i % blocks_per_seq, 0))
    full = lambda a: pl.BlockSpec(a.shape, lambda i: (0, 0))
    wide = HEADS * 2 * LANES
    return pl.pallas_call(
        _mla_proj_kernel,
        name="mla_proj",
        grid=(t // TM_MLA,),
        in_specs=[pl.BlockSpec((TM_MLA, D_MODEL), lambda i: (i, 0)),
                  full(g), full(wb), full(gq), full(wq), full(gkv), full(wkv),
                  tab_spec, tab_spec, tab_spec],
        out_specs=[pl.BlockSpec((TM_MLA, wide), lambda i: (i, 0)),
                   pl.BlockSpec((TM_MLA, wide), lambda i: (i, 0)),
                   pl.BlockSpec((TM_MLA, A_WIDTH), lambda i: (i, 0))],
        out_shape=[jax.ShapeDtypeStruct((t, wide), BF16),
                   jax.ShapeDtypeStruct((t, wide), BF16),
                   jax.ShapeDtypeStruct((t, A_WIDTH), BF16)],
        compiler_params=_params("parallel"),
    )(h, g, wb, gq, wq, gkv, wkv, *tables)


def _chunk_of(pos):
    return jnp.right_shift(pos, CHUNK.bit_length() - 1)


def _chunk_causal_mask():
    qc = _chunk_of(lax.broadcasted_iota(jnp.int32, (TQ, TQ), 0))
    kc = _chunk_of(lax.broadcasted_iota(jnp.int32, (TQ, TQ), 1))
    return kc <= qc


def _causal_weights(s, qs, mask, c):
    s_diag = jnp.where(mask, s[:, qs:], NEG_INF)
    m = jnp.max(s_diag, axis=-1, keepdims=True)
    if qs == 0:
        return [jnp.exp2((s_diag - m) * c)]
    s_off = s[:, :qs]
    m = jnp.maximum(m, jnp.max(s_off, axis=-1, keepdims=True))
    return [jnp.exp2((s_off - m) * c), jnp.exp2((s_diag - m) * c)]


def _as_bf16_row(parts):
    parts = [x.astype(BF16) for x in parts]
    return parts[0] if len(parts) == 1 else jnp.concatenate(parts, axis=1)


def _fill_value_and_ones(vaug_ref, v_ref):
    vaug_ref[:, :LANES] = v_ref[...]
    vaug_ref[:, LANES:] = jnp.ones((v_ref.shape[0], LANES), BF16)


def _grid_cast(src, layer, batch):
    rows, cols = src.shape[1:]
    if cols % (HEADS * LANES) == 0:
        tile, index = (rows // batch, cols // HEADS), lambda b, h: (b, h)
    else:
        tile, index = (rows // (batch * HEADS), cols), lambda b, h: (b * HEADS + h, 0)
    return (pl.BlockSpec((None,) + tile, lambda b, h: (layer,) + index(b, h)),
            pl.BlockSpec(tile, index), jax.ShapeDtypeStruct((rows, cols), BF16))


def _mla_attn_kernel(*refs, seq, scale, cast):
    if cast:
        q_ref, k_ref, v_ref, src_ref, o_ref, dst_ref, vaug_ref = refs
        dst_ref[...] = src_ref[...].astype(BF16)
    else:
        q_ref, k_ref, v_ref, o_ref, vaug_ref = refs
    mask = _chunk_causal_mask()
    _fill_value_and_ones(vaug_ref, v_ref)
    scores = lambda qs: _dot_t(q_ref[qs:qs + TQ, :], k_ref[0:qs + TQ, :])
    s_next = scores(0)
    for qs in range(0, seq, TQ):
        qe = qs + TQ
        s, s_next = s_next, (scores(qe) if qe < seq else None)
        p = _as_bf16_row(_causal_weights(s, qs, mask, scale * LOG2E))
        o = _dot(p, vaug_ref[0:qe, :])
        o_ref[qs:qe, :] = (o[:, :LANES] / o[:, LANES:]).astype(o_ref.dtype)


def _mla_attn(q, k, v, batch, seq, cast=None):
    t = q.shape[0]
    scale = (B_NOPE_DIM + B_ROPE_DIM) ** -0.5
    in_specs = [pl.BlockSpec((seq, 2 * LANES), lambda b, h: (b, h)),
                pl.BlockSpec((seq, 2 * LANES), lambda b, h: (b, h)),
                pl.BlockSpec((seq, LANES), lambda b, h: (b, h))]
    out_specs = [pl.BlockSpec((seq, LANES), lambda b, h: (b, h))]
    out_shape = [jax.ShapeDtypeStruct((t, A_WIDTH), BF16)]
    args = [q, k, v]
    if cast is not None:
        src_spec, dst_spec, dst_shape = _grid_cast(*cast, batch)
        in_specs.append(src_spec)
        out_specs.append(dst_spec)
        out_shape.append(dst_shape)
        args.append(cast[0])
    return pl.pallas_call(
        functools.partial(_mla_attn_kernel, seq=seq, scale=scale, cast=cast is not None),
        name="mla_attn",
        grid=(batch, HEADS),
        in_specs=in_specs,
        out_specs=out_specs,
        out_shape=out_shape,
        scratch_shapes=[pltpu.VMEM((seq, 2 * LANES), BF16)],
        compiler_params=_params("parallel", "parallel"),
    )(*args)


def _diff_attn_kernel(*refs, seq, scale, lambda_init, cast):
    q1_ref, q2_ref, k1_ref, k2_ref, v_ref, lq1_ref, lk1_ref, lq2_ref, lk2_ref, gsub_ref = refs[:10]
    if cast:
        src_ref, o_ref, dst_ref = refs[10:]
        dst_ref[...] = src_ref[...].astype(BF16)
    else:
        o_ref, = refs[10:]
    mask = _chunk_causal_mask()
    lam = (jnp.exp(jnp.sum(lq1_ref[...] * lk1_ref[...], axis=-1, keepdims=True))
           - jnp.exp(jnp.sum(lq2_ref[...] * lk2_ref[...], axis=-1, keepdims=True)) + lambda_init)
    c = scale * LOG2E
    scores = lambda qs: (_dot_t(q1_ref[qs:qs + TQ, :], k1_ref[0:qs + TQ, :]),
                         _dot_t(q2_ref[qs:qs + TQ, :], k2_ref[0:qs + TQ, :]))
    s_next = scores(0)
    for qs in range(0, seq, TQ):
        qe = qs + TQ
        (s1, s2), s_next = s_next, (scores(qe) if qe < seq else None)
        w1 = _causal_weights(s1, qs, mask, c)
        w2 = _causal_weights(s2, qs, mask, c)
        l1 = sum(jnp.sum(x, axis=-1, keepdims=True) for x in w1)
        l2 = sum(jnp.sum(x, axis=-1, keepdims=True) for x in w2)
        o12 = _dot(jnp.concatenate([_as_bf16_row(w1), _as_bf16_row(w2)], axis=0), v_ref[0:qe, :])
        o = o12[:TQ] * (1.0 / l1) - o12[TQ:] * (lam / l2)
        o_ref[qs:qe, :] = (_rms(o, gsub_ref[...]) * (1.0 - lambda_init)).astype(o_ref.dtype)


def _diff_attn(u, lq1, lk1, lq2, lk2, g_sub, lambda_init, batch, seq, cast=None):
    t = u.shape[0]
    groups = C_HEAD_IN // LANES
    part = lambda c: pl.BlockSpec((seq, LANES), lambda b, h: (b, groups * h + c))
    vec = pl.BlockSpec((1, LANES), lambda b, h: (0, 0))
    in_specs = [part(0), part(1), part(2), part(3),
                pl.BlockSpec((seq, C_V_DIM), lambda b, h: (b, (C_HEAD_IN // C_V_DIM) * h + 2)),
                vec, vec, vec, vec,
                pl.BlockSpec((1, C_V_DIM), lambda b, h: (0, 0))]
    out_specs = [pl.BlockSpec((seq, C_V_DIM), lambda b, h: (b, h))]
    out_shape = [jax.ShapeDtypeStruct((t, HEADS * C_V_DIM), BF16)]
    args = [u, u, u, u, u, lq1, lk1, lq2, lk2, g_sub]
    if cast is not None:
        src_spec, dst_spec, dst_shape = _grid_cast(*cast, batch)
        in_specs.append(src_spec)
        out_specs.append(dst_spec)
        out_shape.append(dst_shape)
        args.append(cast[0])
    return pl.pallas_call(
        functools.partial(_diff_attn_kernel, seq=seq, scale=HEAD_DIM ** -0.5, lambda_init=lambda_init,
                          cast=cast is not None),
        name="diff_attn",
        grid=(batch, HEADS),
        in_specs=in_specs,
        out_specs=out_specs,
        out_shape=out_shape,
        compiler_params=_params("parallel", "parallel"),
    )(*args)


def _band_bias_kernel(tab_ref, o_ref, *, inv_scale):
    hd = pl.program_id(0)
    first = N_PREV_CHUNKS * CHUNK
    kj = lax.broadcasted_iota(jnp.int32, (8, BIAS_ROW), 1)
    dist = jnp.where(kj < A_WIN, first - kj, first + BIAS_ROW - kj)
    idx = jnp.clip(dist, -REL_FUTURE, REL_PAST_CLIP) + REL_FUTURE

    def body(tt, acc):
        return jnp.where(idx == tt, tab_ref[hd, tt], acc)

    base = lax.fori_loop(0, REL_TABLE, body, jnp.zeros((8, BIAS_ROW), F32))
    sub = lax.broadcasted_iota(jnp.int32, (8, BIAS_ROW), 0)
    for bit in range(3):
        rotated = pltpu.roll(base, 1 << bit, 1)
        base = jnp.where((jnp.right_shift(sub, bit) & 1) == 1, rotated, base)
    bias = jnp.concatenate([pltpu.roll(base, 8 * b, 1) if b else base for b in range(TQ // 8)], axis=0)
    qc = _chunk_of(lax.broadcasted_iota(jnp.int32, (TQ, A_WIN), 0))
    kc = _chunk_of(lax.broadcasted_iota(jnp.int32, (TQ, A_WIN), 1))
    valid = (kc >= qc) & (kc <= qc + N_PREV_CHUNKS)
    o_ref[0] = jnp.where(valid, bias[:, :A_WIN] * inv_scale, NEG_INF)


def _band_bias(rel_table, scale):
    assert BIAS_ROW >= A_WIN + TQ
    return pl.pallas_call(
        functools.partial(_band_bias_kernel, inv_scale=1.0 / scale),
        name="band_bias",
        grid=(HEADS,),
        in_specs=[pl.BlockSpec(memory_space=pltpu.SMEM)],
        out_specs=pl.BlockSpec((1, TQ, A_WIN), lambda h: (h, 0, 0)),
        out_shape=jax.ShapeDtypeStruct((HEADS, TQ, A_WIN), F32),
        compiler_params=_params("parallel"),
    )(rel_table)


def _band_attn_kernel(q_ref, k_ref, v_ref, bias_ref, o_ref, vaug_ref, *, seq, scale):
    _fill_value_and_ones(vaug_ref, v_ref)
    window = lambda qs: max(0, qs - N_PREV_CHUNKS * CHUNK)
    scores = lambda qs: _dot_t(q_ref[qs:qs + TQ, :], k_ref[window(qs):qs + TQ, :])
    s_next = scores(0)
    for qs in range(0, seq, TQ):
        qe, ws = qs + TQ, window(qs)
        s, s_next = s_next, (scores(qe) if qe < seq else None)
        s = s + bias_ref[0, :, A_WIN - (qe - ws):]
        p = jnp.exp2((s - jnp.max(s, axis=-1, keepdims=True)) * (scale * LOG2E))
        o = _dot(p.astype(BF16), vaug_ref[ws:qe, :])
        o_ref[qs:qe, :] = (o[:, :LANES] / o[:, LANES:]).astype(o_ref.dtype)


def _band_attn(u, rel_table, batch, seq):
    t = u.shape[0]
    scale = HEAD_DIM ** -0.5
    part = lambda c: pl.BlockSpec((seq, LANES), lambda b, h: (b, c * HEADS + h))
    return pl.pallas_call(
        functools.partial(_band_attn_kernel, seq=seq, scale=scale),
        name="band_attn",
        grid=(batch, HEADS),
        in_specs=[part(0), part(1), part(2),
                  pl.BlockSpec((1, TQ, A_WIN), lambda b, h: (h, 0, 0))],
        out_specs=pl.BlockSpec((seq, LANES), lambda b, h: (b, h)),
        out_shape=jax.ShapeDtypeStruct((t, A_WIDTH), BF16),
        scratch_shapes=[pltpu.VMEM((seq, 2 * LANES), BF16)],
        compiler_params=_params("parallel", "parallel"),
    )(u, u, u, _band_bias(rel_table, scale))


def _out_proj_kernel(*refs, n_in):
    x_refs, (w_ref, g_ref, h_ref, o_ref) = refs[:n_in], refs[n_in:]

    def project(rows):
        mix = None
        k0 = 0
        for x_ref in x_refs:
            k1 = k0 + x_ref.shape[1]
            part = _dot(x_ref[rows, :], w_ref[k0:k1, :])
            mix = part if mix is None else mix + part
            k0 = k1
        return mix

    def finish(rows, mix):
        o_ref[rows, :] = h_ref[rows, :] + _rms(mix, g_ref[...])

    _pipelined(_row_chunks(TM_PROJ, PROJ_ROW_CHUNKS), lambda rows: None, project, finish)


def _out_proj(xs, w, g, h):
    t = h.shape[0]
    return pl.pallas_call(
        functools.partial(_out_proj_kernel, n_in=len(xs)),
        name="out_proj",
        grid=(t // TM_PROJ,),
        in_specs=[pl.BlockSpec((TM_PROJ, x.shape[1]), lambda i: (i, 0)) for x in xs] + [
            pl.BlockSpec(w.shape, lambda i: (0, 0), pipeline_mode=pl.Buffered(1)),
            pl.BlockSpec((1, D_MODEL), lambda i: (0, 0)),
            pl.BlockSpec((TM_PROJ, D_MODEL), lambda i: (i, 0)),
        ],
        out_specs=pl.BlockSpec((TM_PROJ, D_MODEL), lambda i: (i, 0)),
        out_shape=jax.ShapeDtypeStruct((t, D_MODEL), F32),
        compiler_params=_params("parallel"),
    )(*xs, w, g, h)


def _ple_kernel(h_ref, gpre_ref, wg_ref, p_ref, wp_ref, gpost_ref, o_ref, hn_ref):
    def pre_norm(rows):
        hn_ref[rows, :] = _rms(h_ref[rows, :], gpre_ref[...]).astype(BF16)

    def project(rows):
        return _dot(hn_ref[rows, :], wg_ref[...]), _dot(p_ref[rows, :].astype(BF16), wp_ref[...])

    def finish(rows, gate_emb):
        gate, emb = gate_emb
        o_ref[rows, :] = h_ref[rows, :] + _rms(jax.nn.sigmoid(gate) * emb, gpost_ref[...])

    _pipelined(_row_chunks(TM_PROJ, PROJ_ROW_CHUNKS), pre_norm, project, finish)


def _ple(h, g_pre, w_gate, p, w_proj, g_post, layer):
    t = h.shape[0]
    return pl.pallas_call(
        _ple_kernel,
        name="gated_embed",
        grid=(t // TM_PROJ,),
        in_specs=[
            pl.BlockSpec((TM_PROJ, D_MODEL), lambda i: (i, 0)),
            pl.BlockSpec((1, D_MODEL), lambda i: (0, 0)),
            pl.BlockSpec(w_gate.shape, lambda i: (0, 0), pipeline_mode=pl.Buffered(1)),
            pl.BlockSpec((None, TM_PROJ, PLE_DIM), lambda i: (layer, i, 0)),
            pl.BlockSpec(w_proj.shape, lambda i: (0, 0), pipeline_mode=pl.Buffered(1)),
            pl.BlockSpec((1, D_MODEL), lambda i: (0, 0)),
        ],
        out_specs=pl.BlockSpec((TM_PROJ, D_MODEL), lambda i: (i, 0)),
        out_shape=jax.ShapeDtypeStruct((t, D_MODEL), F32),
        scratch_shapes=[pltpu.VMEM((TM_PROJ, D_MODEL), BF16)],
        compiler_params=_params("parallel"),
    )(h, g_pre, w_gate, p, w_proj, g_post)


def _rope_tables(seq, rot_dim):
    inv = ROPE_THETA ** (-jnp.arange(0, rot_dim, 2, dtype=F32) / rot_dim)
    ang = jnp.arange(seq, dtype=F32)[:, None] * inv[None, :]
    return jnp.cos(ang), jnp.sin(ang)


def _lane_tables(cos, sin, tail):
    seq, half = cos.shape
    rest = LANES - 2 * half
    zeros = lambda n: jnp.zeros((seq, n), F32)
    cos_t = jnp.concatenate([cos, cos, jnp.full((seq, rest), tail, F32)], axis=1)
    sin_up = jnp.concatenate([zeros(half), sin, zeros(rest)], axis=1)
    sin_dn = jnp.concatenate([-sin, zeros(LANES - half)], axis=1)
    return cos_t, sin_up, sin_dn


def _row(v):
    return v.reshape(1, -1)


def kernel(x, p, ffn1_g_pre, ffn1_w_in, ffn1_w_out, ffn1_g_post, mix_g_pre, mix_g_post, ab_w_in, a_rel_bias, b_g_q, b_w_qup, b_g_kv, b_w_kvup, ab_w_out, c_w_in, c_lq1, c_lk1, c_lq2, c_lk2, c_g_sub, c_w_out, ffn2_g_pre, ffn2_w_in, ffn2_w_out, ffn2_g_post, ple_g_pre, ple_w_gate, ple_w_proj, ple_g_post):
    batch, seq, _ = x.shape
    t = batch * seq
    assert seq % TQ == 0 and seq % TM_IN == 0 and seq % TM_PROJ == 0 and t % TM_FFN == 0

    tab_b = _lane_tables(*_rope_tables(seq, B_ROPE_DIM), tail=0.0)
    tab_c = _lane_tables(*_rope_tables(seq, C_ROT_DIM), tail=1.0)

    ffn_f32 = [(w_in, w_out, i) for i in range(DEPTH)
               for w_in, w_out in ((ffn1_w_in, ffn1_w_out), (ffn2_w_in, ffn2_w_out))]
    ffn_f32.append(None)
    ffn_w = (ffn1_w_in[0].astype(BF16), ffn1_w_out[0].astype(BF16))
    mix_in = [ab_w_in, c_w_in]
    mix_out = [ab_w_out, c_w_out]
    w_mix_in = ab_w_in[0].astype(BF16)
    pad_rope = LANES - B_ROPE_DIM
    w_q = b_w_qup.astype(BF16).reshape(-1, B_Q_LORA, HEADS, B_NOPE_DIM + B_ROPE_DIM)
    w_q = jnp.pad(w_q, ((0, 0), (0, 0), (0, 0), (0, pad_rope))).reshape(-1, B_Q_LORA, HEADS * 2 * LANES)
    b_w_kvup = b_w_kvup.astype(BF16)

    plain = (False,) * 8
    c_groups = (True,) * 4 + (False,) * 2

    h = x.reshape(t, D_MODEL)
    p = p.reshape(DEPTH, t, PLE_DIM)
    for i in range(DEPTH):
        j = i // 2
        h, *ffn_w = _ffn(h, _row(ffn1_g_pre[i]), *ffn_w, _row(ffn1_g_post[i]), ffn_f32[2 * i + 1])
        g_mix = _row(mix_g_pre[i])
        layer_casts = [(mix_out[i % 2], j), (ple_w_gate, i), (ple_w_proj, i)]
        next_in = (mix_in[(i + 1) % 2], (i + 1) // 2) if i + 1 < DEPTH else None
        if i % 2 == 0:
            u_a, w_out, w_gate, w_emb = _proj(h, g_mix, w_mix_in, 3 * A_WIDTH, tab_b, seq, plain, 0,
                                              layer_casts)
            w_b = jnp.pad(w_mix_in[:, 3 * A_WIDTH:], ((0, 0), (0, pad_rope)))
            q_b, k_b, v_b = _mla_proj(h, g_mix, w_b, _row(b_g_q[j]), w_q[j], _row(b_g_kv[j]),
                                      b_w_kvup[j], tab_b, seq)
            o_a = _band_attn(u_a, a_rel_bias[j], batch, seq)
            o_b, *w_next = _mla_attn(q_b, k_b, v_b, batch, seq, next_in)
            h = _out_proj([o_a, o_b], w_out, _row(mix_g_post[i]), h)
        else:
            lambda_init = 0.8 - 0.6 * math.exp(-0.3 * i)
            u_c, w_out, w_gate, w_emb = _proj(h, g_mix, w_mix_in, HEADS * C_HEAD_IN, tab_c, seq, c_groups,
                                              C_ROT_DIM // 2, layer_casts)
            o_c, *w_next = _diff_attn(u_c, _row(c_lq1[j]), _row(c_lk1[j]), _row(c_lq2[j]), _row(c_lk2[j]),
                                      _row(c_g_sub[j]), lambda_init, batch, seq, next_in)
            h = _out_proj([o_c], w_out, _row(mix_g_post[i]), h)
        if w_next:
            w_mix_in, = w_next
        h, *ffn_w = _ffn(h, _row(ffn2_g_pre[i]), *ffn_w, _row(ffn2_g_post[i]), ffn_f32[2 * i + 2])
        h = _ple(h, _row(ple_g_pre[i]), w_gate, p, w_emb, _row(ple_g_post[i]), i)
    return h.reshape(batch, seq, D_MODEL)
```

```python
import functools
import math

import jax
import jax.numpy as jnp
from jax import lax
from jax.experimental import pallas as pl
from jax.experimental.pallas import tpu as pltpu

D_MODEL = 2048
DEPTH = 4
CHUNK = 64
N_PREV_CHUNKS = 8
ROPE_THETA = 500000.0
PLE_DIM = 256
D_FF = 5632
NORM_EPS = 1e-6
NEG_INF = -1e30
LOG2E = math.log2(math.e)

HEADS = 8
HEAD_DIM = 128
A_WIDTH = HEADS * HEAD_DIM
REL_FUTURE = CHUNK - 1
REL_PAST_CLIP = 128
REL_TABLE = REL_FUTURE + REL_PAST_CLIP + 1

B_NOPE_DIM = 128
B_ROPE_DIM = 64
B_Q_LORA = 512
B_KV_LORA = 256

C_V_DIM = 2 * HEAD_DIM
C_ROT_DIM = HEAD_DIM // 4
C_HEAD_IN = 4 * HEAD_DIM + C_V_DIM

LANES = 128
VMEM_LIMIT = 58 * 1024 * 1024

FFN_VMEM_LIMIT = 62 * 1024 * 1024

TM_FFN = 1024
TF_FFN = 512
FFN_ROW_CHUNKS = 4
TM_IN = 1024
IN_ROW_CHUNKS = 4
PROJ_ROW_CHUNKS = 2
TM_PROJ = 512
TQ = 256
A_WIN = N_PREV_CHUNKS * CHUNK + TQ
BIAS_ROW = 1024

BF16 = jnp.bfloat16
F32 = jnp.float32


def _rms(x, g):
    return x * lax.rsqrt(jnp.mean(x * x, axis=-1, keepdims=True) + NORM_EPS) * g


def _dot(a, b):
    return jnp.dot(a, b, preferred_element_type=F32)


def _dot_t(a, b):
    return lax.dot_general(a, b, (((1,), (1,)), ((), ())), preferred_element_type=F32)


def _rope_group(u, cos, sin_up, sin_dn, half):
    return (u * cos + pltpu.roll(u, half, 1) * sin_up
            + pltpu.roll(u, LANES - half, 1) * sin_dn)


def _row_chunks(rows, n):
    return [slice(r, r + rows // n) for r in range(0, rows, rows // n)]


def _pipelined(chunks, pre, mm, epi):
    n = len(chunks)
    for rows in chunks[:2]:
        pre(rows)
    nxt = mm(chunks[0])
    for c, rows in enumerate(chunks):
        cur = nxt
        if c + 2 < n:
            pre(chunks[c + 2])
        if c + 1 < n:
            nxt = mm(chunks[c + 1])
        epi(rows, cur)


def _params(*sem):
    return pltpu.CompilerParams(dimension_semantics=sem, vmem_limit_bytes=VMEM_LIMIT)


def _ffn_kernel(*refs, cast_next):
    if cast_next:
        (h_ref, gpre_ref, wa_ref, wb_ref, wo_ref, gpost_ref, nin_ref, nout_ref,
         o_ref, nin_bf_ref, nout_bf_ref, hn_ref) = refs
        nin_bf_ref[...] = nin_ref[...].astype(BF16)
        nout_bf_ref[...] = nout_ref[...].astype(BF16)
    else:
        h_ref, gpre_ref, wa_ref, wb_ref, wo_ref, gpost_ref, o_ref, hn_ref = refs
    j = pl.program_id(1)
    last = pl.num_programs(1) - 1
    chunks = _row_chunks(TM_FFN, FFN_ROW_CHUNKS)

    def swiglu(rows):
        hn = hn_ref[rows, :]
        a = _dot(hn, wa_ref[...])
        b = _dot(hn, wb_ref[...])
        return _dot((a * jax.nn.sigmoid(a) * b).astype(BF16), wo_ref[...])

    def pre_norm(rows):
        hn_ref[rows, :] = _rms(h_ref[rows, :], gpre_ref[...]).astype(BF16)

    def first(rows, part):
        o_ref[rows, :] = part

    def finish(rows, part):
        o_ref[rows, :] = h_ref[rows, :] + 0.5 * _rms(o_ref[rows, :] + part, gpost_ref[...])

    @pl.when(j == 0)
    def _():
        _pipelined(chunks, pre_norm, swiglu, first)

    @pl.when((j > 0) & (j < last))
    def _():
        o_ref[...] += swiglu(slice(None))

    @pl.when(j == last)
    def _():
        _pipelined(chunks, lambda rows: None, swiglu, finish)


def _ffn(h, g_pre, w_in, w_out, g_post, cast_next=None):
    t = h.shape[0]
    ni, nf = t // TM_FFN, D_FF // TF_FFN
    assert nf >= 3
    in_specs = [
        pl.BlockSpec((TM_FFN, D_MODEL), lambda i, j: (i, 0)),
        pl.BlockSpec((1, D_MODEL), lambda i, j: (0, 0)),
        pl.BlockSpec((D_MODEL, TF_FFN), lambda i, j: (0, j)),
        pl.BlockSpec((D_MODEL, TF_FFN), lambda i, j: (0, j + nf)),
        pl.BlockSpec((TF_FFN, D_MODEL), lambda i, j: (j, 0)),
        pl.BlockSpec((1, D_MODEL), lambda i, j: (0, 0)),
    ]
    out_specs = [pl.BlockSpec((TM_FFN, D_MODEL), lambda i, j: (i, 0))]
    out_shape = [jax.ShapeDtypeStruct((t, D_MODEL), F32)]
    args = [h, g_pre, w_in, w_in, w_out, g_post]
    if cast_next is not None:
        nxt_in, nxt_out, layer = cast_next
        in_tile = (D_MODEL // ni, 2 * D_FF // nf)
        out_tile = (D_FF // nf, D_MODEL // ni)
        in_specs += [pl.BlockSpec((None,) + in_tile, lambda i, j: (layer, i, j)),
                     pl.BlockSpec((None,) + out_tile, lambda i, j: (layer, j, i))]
        out_specs += [pl.BlockSpec(in_tile, lambda i, j: (i, j)),
                      pl.BlockSpec(out_tile, lambda i, j: (j, i))]
        out_shape += [jax.ShapeDtypeStruct(nxt_in.shape[1:], BF16),
                      jax.ShapeDtypeStruct(nxt_out.shape[1:], BF16)]
        args += [nxt_in, nxt_out]
    return pl.pallas_call(
        functools.partial(_ffn_kernel, cast_next=cast_next is not None),
        name="ffn",
        grid=(ni, nf),
        in_specs=in_specs,
        out_specs=out_specs,
        out_shape=out_shape,
        scratch_shapes=[pltpu.VMEM((TM_FFN, D_MODEL), BF16)],
        compiler_params=pltpu.CompilerParams(dimension_semantics=("parallel", "arbitrary"),
                                             vmem_limit_bytes=FFN_VMEM_LIMIT),
    )(*args)


def _proj_kernel(h_ref, g_ref, w_ref, cos_ref, sup_ref, sdn_ref, o_ref, hn_ref, *, rope_groups, half):
    chunks = _row_chunks(TM_IN, IN_ROW_CHUNKS)

    def pre_norm(rows):
        hn_ref[rows, :] = _rms(h_ref[rows, :], g_ref[...]).astype(BF16)

    def project(rows):
        return _dot(hn_ref[rows, :], w_ref[...])

    def emit(rows, u):
        if not any(rope_groups):
            o_ref[rows, :] = u.astype(o_ref.dtype)
            return
        cos, sup, sdn = cos_ref[rows, :], sup_ref[rows, :], sdn_ref[rows, :]
        for c, rotate in enumerate(rope_groups):
            sl = slice(c * LANES, (c + 1) * LANES)
            x = _rope_group(u[:, sl], cos, sup, sdn, half) if rotate else u[:, sl]
            o_ref[rows, sl] = x.astype(o_ref.dtype)

    @pl.when(pl.program_id(1) == 0)
    def _():
        _pipelined(chunks, pre_norm, project, emit)

    @pl.when(pl.program_id(1) > 0)
    def _():
        _pipelined(chunks, lambda rows: None, project, emit)


def _proj(h, g, w, layer, n_cols, tables, seq, rope_groups, half):
    t = h.shape[0]
    tn = len(rope_groups) * LANES
    blocks_per_seq = seq // TM_IN
    tab_spec = pl.BlockSpec((TM_IN, LANES), lambda i, j: (i % blocks_per_seq, 0))
    return pl.pallas_call(
        functools.partial(_proj_kernel, rope_groups=rope_groups, half=half),
        name="in_proj",
        grid=(t // TM_IN, n_cols // tn),
        in_specs=[
            pl.BlockSpec((TM_IN, D_MODEL), lambda i, j: (i, 0)),
            pl.BlockSpec((1, D_MODEL), lambda i, j: (0, 0)),
            (pl.BlockSpec((D_MODEL, tn), lambda i, j: (0, j)) if layer is None else
             pl.BlockSpec((None, D_MODEL, tn), lambda i, j: (layer, 0, j))),
            tab_spec, tab_spec, tab_spec,
        ],
        out_specs=pl.BlockSpec((TM_IN, tn), lambda i, j: (i, j)),
        out_shape=jax.ShapeDtypeStruct((t, n_cols), BF16),
        scratch_shapes=[pltpu.VMEM((TM_IN, D_MODEL), BF16)],
        compiler_params=_params("parallel", "arbitrary"),
    )(h, g, w, *tables)


def _mla_proj_kernel(h_ref, g_ref, wb_ref, gq_ref, wq_ref, gkv_ref, wkv_ref,
                     cos_ref, sup_ref, sdn_ref, q_ref, k_ref, v_ref):
    hn = _rms(h_ref[...], g_ref[...]).astype(BF16)
    u = _dot(hn, wb_ref[...])
    cq = u[:, :B_Q_LORA]
    ckv = u[:, B_Q_LORA:B_Q_LORA + B_KV_LORA]
    kr = u[:, B_Q_LORA + B_KV_LORA:]
    cos, sup, sdn = cos_ref[...], sup_ref[...], sdn_ref[...]
    half = B_ROPE_DIM // 2

    q = _dot(_rms(cq, gq_ref[...]).astype(BF16), wq_ref[...])
    kv = _dot(_rms(ckv, gkv_ref[...]).astype(BF16), wkv_ref[...])
    k_rope = _rope_group(kr, cos, sup, sdn, half).astype(BF16)
    for hd in range(HEADS):
        lo, mid, hi = 2 * hd * LANES, (2 * hd + 1) * LANES, (2 * hd + 2) * LANES
        q_ref[:, lo:mid] = q[:, lo:mid].astype(BF16)
        q_ref[:, mid:hi] = _rope_group(q[:, mid:hi], cos, sup, sdn, half).astype(BF16)
        k_ref[:, lo:mid] = kv[:, lo:mid].astype(BF16)
        k_ref[:, mid:hi] = k_rope
        v_ref[:, hd * LANES:(hd + 1) * LANES] = kv[:, mid:hi].astype(BF16)


def _mla_proj(h, g, wb, gq, wq, gkv, wkv, tables, seq):
    t = h.shape[0]
    blocks_per_seq = seq // TM_PROJ
    tab_spec = pl.BlockSpec((TM_PROJ, LANES), lambda i: (i % blocks_per_seq, 0))
    full = lambda a: pl.BlockSpec(a.shape, lambda i: (0, 0))
    wide = HEADS * 2 * LANES
    return pl.pallas_call(
        _mla_proj_kernel,
        name="mla_proj",
        grid=(t // TM_PROJ,),
        in_specs=[pl.BlockSpec((TM_PROJ, D_MODEL), lambda i: (i, 0)),
                  full(g), full(wb), full(gq), full(wq), full(gkv), full(wkv),
                  tab_spec, tab_spec, tab_spec],
        out_specs=[pl.BlockSpec((TM_PROJ, wide), lambda i: (i, 0)),
                   pl.BlockSpec((TM_PROJ, wide), lambda i: (i, 0)),
                   pl.BlockSpec((TM_PROJ, A_WIDTH), lambda i: (i, 0))],
        out_shape=[jax.ShapeDtypeStruct((t, wide), BF16),
                   jax.ShapeDtypeStruct((t, wide), BF16),
                   jax.ShapeDtypeStruct((t, A_WIDTH), BF16)],
        compiler_params=_params("parallel"),
    )(h, g, wb, gq, wq, gkv, wkv, *tables)


def _chunk_of(pos):
    return jnp.right_shift(pos, CHUNK.bit_length() - 1)


def _chunk_causal_mask():
    qc = _chunk_of(lax.broadcasted_iota(jnp.int32, (TQ, TQ), 0))
    kc = _chunk_of(lax.broadcasted_iota(jnp.int32, (TQ, TQ), 1))
    return kc <= qc


def _causal_weights(s, qs, mask, c):
    s_diag = jnp.where(mask, s[:, qs:], NEG_INF)
    m = jnp.max(s_diag, axis=-1, keepdims=True)
    if qs == 0:
        return [jnp.exp2((s_diag - m) * c)]
    s_off = s[:, :qs]
    m = jnp.maximum(m, jnp.max(s_off, axis=-1, keepdims=True))
    return [jnp.exp2((s_off - m) * c), jnp.exp2((s_diag - m) * c)]


def _as_bf16_row(parts):
    parts = [x.astype(BF16) for x in parts]
    return parts[0] if len(parts) == 1 else jnp.concatenate(parts, axis=1)


def _fill_value_and_ones(vaug_ref, v_ref):
    vaug_ref[:, :LANES] = v_ref[...]
    vaug_ref[:, LANES:] = jnp.ones((v_ref.shape[0], LANES), BF16)


def _grid_casts(casts, batch):
    in_specs, out_specs, out_shape = [], [], []
    for src, layer in casts:
        rows, cols = src.shape[1:]
        tile = (rows // batch, cols // HEADS)
        in_specs.append(pl.BlockSpec((None,) + tile, lambda b, h, layer=layer: (layer, b, h)))
        out_specs.append(pl.BlockSpec(tile, lambda b, h: (b, h)))
        out_shape.append(jax.ShapeDtypeStruct((rows, cols), BF16))
    return in_specs, out_specs, out_shape, [src for src, _ in casts]


def _cast_tiles(src_refs, dst_refs):
    for src, dst in zip(src_refs, dst_refs):
        dst[...] = src[...].astype(BF16)


def _mla_attn_kernel(*refs, seq, scale, n_cast):
    q_ref, k_ref, v_ref = refs[:3]
    o_ref, vaug_ref = refs[3 + n_cast], refs[-1]
    _cast_tiles(refs[3:3 + n_cast], refs[4 + n_cast:-1])
    mask = _chunk_causal_mask()
    _fill_value_and_ones(vaug_ref, v_ref)
    scores = lambda qs: _dot_t(q_ref[qs:qs + TQ, :], k_ref[0:qs + TQ, :])
    s_next = scores(0)
    for qs in range(0, seq, TQ):
        qe = qs + TQ
        s, s_next = s_next, (scores(qe) if qe < seq else None)
        p = _as_bf16_row(_causal_weights(s, qs, mask, scale * LOG2E))
        o = _dot(p, vaug_ref[0:qe, :])
        o_ref[qs:qe, :] = (o[:, :LANES] / o[:, LANES:]).astype(o_ref.dtype)


def _mla_attn(q, k, v, batch, seq, casts):
    t = q.shape[0]
    scale = (B_NOPE_DIM + B_ROPE_DIM) ** -0.5
    cast_in, cast_out, cast_shape, cast_args = _grid_casts(casts, batch)
    return pl.pallas_call(
        functools.partial(_mla_attn_kernel, seq=seq, scale=scale, n_cast=len(casts)),
        name="mla_attn",
        grid=(batch, HEADS),
        in_specs=[pl.BlockSpec((seq, 2 * LANES), lambda b, h: (b, h)),
                  pl.BlockSpec((seq, 2 * LANES), lambda b, h: (b, h)),
                  pl.BlockSpec((seq, LANES), lambda b, h: (b, h))] + cast_in,
        out_specs=[pl.BlockSpec((seq, LANES), lambda b, h: (b, h))] + cast_out,
        out_shape=[jax.ShapeDtypeStruct((t, A_WIDTH), BF16)] + cast_shape,
        scratch_shapes=[pltpu.VMEM((seq, 2 * LANES), BF16)],
        compiler_params=_params("parallel", "parallel"),
    )(q, k, v, *cast_args)


def _diff_attn_kernel(*refs, seq, scale, lambda_init, n_cast):
    q1_ref, q2_ref, k1_ref, k2_ref, v_ref, lq1_ref, lk1_ref, lq2_ref, lk2_ref, gsub_ref = refs[:10]
    o_ref = refs[10 + n_cast]
    _cast_tiles(refs[10:10 + n_cast], refs[11 + n_cast:])
    mask = _chunk_causal_mask()
    lam = (jnp.exp(jnp.sum(lq1_ref[...] * lk1_ref[...], axis=-1, keepdims=True))
           - jnp.exp(jnp.sum(lq2_ref[...] * lk2_ref[...], axis=-1, keepdims=True)) + lambda_init)
    c = scale * LOG2E
    scores = lambda qs: (_dot_t(q1_ref[qs:qs + TQ, :], k1_ref[0:qs + TQ, :]),
                         _dot_t(q2_ref[qs:qs + TQ, :], k2_ref[0:qs + TQ, :]))
    s_next = scores(0)
    for qs in range(0, seq, TQ):
        qe = qs + TQ
        (s1, s2), s_next = s_next, (scores(qe) if qe < seq else None)
        w1 = _causal_weights(s1, qs, mask, c)
        w2 = _causal_weights(s2, qs, mask, c)
        l1 = sum(jnp.sum(x, axis=-1, keepdims=True) for x in w1)
        l2 = sum(jnp.sum(x, axis=-1, keepdims=True) for x in w2)
        o12 = _dot(jnp.concatenate([_as_bf16_row(w1), _as_bf16_row(w2)], axis=0), v_ref[0:qe, :])
        o = o12[:TQ] * (1.0 / l1) - o12[TQ:] * (lam / l2)
        o_ref[qs:qe, :] = (_rms(o, gsub_ref[...]) * (1.0 - lambda_init)).astype(o_ref.dtype)


def _diff_attn(u, lq1, lk1, lq2, lk2, g_sub, lambda_init, batch, seq, casts):
    t = u.shape[0]
    groups = C_HEAD_IN // LANES
    part = lambda c: pl.BlockSpec((seq, LANES), lambda b, h: (b, groups * h + c))
    vec = pl.BlockSpec((1, LANES), lambda b, h: (0, 0))
    cast_in, cast_out, cast_shape, cast_args = _grid_casts(casts, batch)
    return pl.pallas_call(
        functools.partial(_diff_attn_kernel, seq=seq, scale=HEAD_DIM ** -0.5, lambda_init=lambda_init,
                          n_cast=len(casts)),
        name="diff_attn",
        grid=(batch, HEADS),
        in_specs=[part(0), part(1), part(2), part(3),
                  pl.BlockSpec((seq, C_V_DIM), lambda b, h: (b, (C_HEAD_IN // C_V_DIM) * h + 2)),
                  vec, vec, vec, vec,
                  pl.BlockSpec((1, C_V_DIM), lambda b, h: (0, 0))] + cast_in,
        out_specs=[pl.BlockSpec((seq, C_V_DIM), lambda b, h: (b, h))] + cast_out,
        out_shape=[jax.ShapeDtypeStruct((t, HEADS * C_V_DIM), BF16)] + cast_shape,
        compiler_params=_params("parallel", "parallel"),
    )(u, u, u, u, u, lq1, lk1, lq2, lk2, g_sub, *cast_args)


def _band_bias_kernel(tab_ref, o_ref, *, inv_scale):
    hd = pl.program_id(0)
    first = N_PREV_CHUNKS * CHUNK
    kj = lax.broadcasted_iota(jnp.int32, (8, BIAS_ROW), 1)
    dist = jnp.where(kj < A_WIN, first - kj, first + BIAS_ROW - kj)
    idx = jnp.clip(dist, -REL_FUTURE, REL_PAST_CLIP) + REL_FUTURE

    def body(tt, acc):
        return jnp.where(idx == tt, tab_ref[hd, tt], acc)

    base = lax.fori_loop(0, REL_TABLE, body, jnp.zeros((8, BIAS_ROW), F32))
    sub = lax.broadcasted_iota(jnp.int32, (8, BIAS_ROW), 0)
    for bit in range(3):
        rotated = pltpu.roll(base, 1 << bit, 1)
        base = jnp.where((jnp.right_shift(sub, bit) & 1) == 1, rotated, base)
    bias = jnp.concatenate([pltpu.roll(base, 8 * b, 1) if b else base for b in range(TQ // 8)], axis=0)
    qc = _chunk_of(lax.broadcasted_iota(jnp.int32, (TQ, A_WIN), 0))
    kc = _chunk_of(lax.broadcasted_iota(jnp.int32, (TQ, A_WIN), 1))
    valid = (kc >= qc) & (kc <= qc + N_PREV_CHUNKS)
    o_ref[0] = jnp.where(valid, bias[:, :A_WIN] * inv_scale, NEG_INF)


def _band_bias(rel_table, scale):
    assert BIAS_ROW >= A_WIN + TQ
    return pl.pallas_call(
        functools.partial(_band_bias_kernel, inv_scale=1.0 / scale),
        name="band_bias",
        grid=(HEADS,),
        in_specs=[pl.BlockSpec(memory_space=pltpu.SMEM)],
        out_specs=pl.BlockSpec((1, TQ, A_WIN), lambda h: (h, 0, 0)),
        out_shape=jax.ShapeDtypeStruct((HEADS, TQ, A_WIN), F32),
        compiler_params=_params("parallel"),
    )(rel_table)


def _band_attn_kernel(*refs, seq, scale, n_cast):
    q_ref, k_ref, v_ref, bias_ref = refs[:4]
    o_ref, vaug_ref = refs[4 + n_cast], refs[-1]
    _cast_tiles(refs[4:4 + n_cast], refs[5 + n_cast:-1])
    _fill_value_and_ones(vaug_ref, v_ref)
    window = lambda qs: max(0, qs - N_PREV_CHUNKS * CHUNK)
    scores = lambda qs: _dot_t(q_ref[qs:qs + TQ, :], k_ref[window(qs):qs + TQ, :])
    s_next = scores(0)
    for qs in range(0, seq, TQ):
        qe, ws = qs + TQ, window(qs)
        s, s_next = s_next, (scores(qe) if qe < seq else None)
        s = s + bias_ref[0, :, A_WIN - (qe - ws):]
        p = jnp.exp2((s - jnp.max(s, axis=-1, keepdims=True)) * (scale * LOG2E))
        o = _dot(p.astype(BF16), vaug_ref[ws:qe, :])
        o_ref[qs:qe, :] = (o[:, :LANES] / o[:, LANES:]).astype(o_ref.dtype)


def _band_attn(u, rel_table, batch, seq, casts):
    t = u.shape[0]
    scale = HEAD_DIM ** -0.5
    part = lambda c: pl.BlockSpec((seq, LANES), lambda b, h: (b, c * HEADS + h))
    cast_in, cast_out, cast_shape, cast_args = _grid_casts(casts, batch)
    return pl.pallas_call(
        functools.partial(_band_attn_kernel, seq=seq, scale=scale, n_cast=len(casts)),
        name="band_attn",
        grid=(batch, HEADS),
        in_specs=[part(0), part(1), part(2),
                  pl.BlockSpec((1, TQ, A_WIN), lambda b, h: (h, 0, 0))] + cast_in,
        out_specs=[pl.BlockSpec((seq, LANES), lambda b, h: (b, h))] + cast_out,
        out_shape=[jax.ShapeDtypeStruct((t, A_WIDTH), BF16)] + cast_shape,
        scratch_shapes=[pltpu.VMEM((seq, 2 * LANES), BF16)],
        compiler_params=_params("parallel", "parallel"),
    )(u, u, u, _band_bias(rel_table, scale), *cast_args)


def _out_proj_kernel(*refs, n_in):
    x_refs, (w_ref, g_ref, h_ref, o_ref) = refs[:n_in], refs[n_in:]

    def project(rows):
        mix = None
        k0 = 0
        for x_ref in x_refs:
            k1 = k0 + x_ref.shape[1]
            part = _dot(x_ref[rows, :], w_ref[k0:k1, :])
            mix = part if mix is None else mix + part
            k0 = k1
        return mix

    def finish(rows, mix):
        o_ref[rows, :] = h_ref[rows, :] + _rms(mix, g_ref[...])

    _pipelined(_row_chunks(TM_PROJ, PROJ_ROW_CHUNKS), lambda rows: None, project, finish)


def _out_proj(xs, w, g, h):
    t = h.shape[0]
    return pl.pallas_call(
        functools.partial(_out_proj_kernel, n_in=len(xs)),
        name="out_proj",
        grid=(t // TM_PROJ,),
        in_specs=[pl.BlockSpec((TM_PROJ, x.shape[1]), lambda i: (i, 0)) for x in xs] + [
            pl.BlockSpec(w.shape, lambda i: (0, 0), pipeline_mode=pl.Buffered(1)),
            pl.BlockSpec((1, D_MODEL), lambda i: (0, 0)),
            pl.BlockSpec((TM_PROJ, D_MODEL), lambda i: (i, 0)),
        ],
        out_specs=pl.BlockSpec((TM_PROJ, D_MODEL), lambda i: (i, 0)),
        out_shape=jax.ShapeDtypeStruct((t, D_MODEL), F32),
        compiler_params=_params("parallel"),
    )(*xs, w, g, h)


def _ple_kernel(h_ref, gpre_ref, wg_ref, p_ref, wp_ref, gpost_ref, o_ref, hn_ref):
    def pre_norm(rows):
        hn_ref[rows, :] = _rms(h_ref[rows, :], gpre_ref[...]).astype(BF16)

    def project(rows):
        return _dot(hn_ref[rows, :], wg_ref[...]), _dot(p_ref[rows, :].astype(BF16), wp_ref[...])

    def finish(rows, gate_emb):
        gate, emb = gate_emb
        o_ref[rows, :] = h_ref[rows, :] + _rms(jax.nn.sigmoid(gate) * emb, gpost_ref[...])

    _pipelined(_row_chunks(TM_PROJ, PROJ_ROW_CHUNKS), pre_norm, project, finish)


def _ple(h, g_pre, w_gate, p, w_proj, g_post, layer):
    t = h.shape[0]
    return pl.pallas_call(
        _ple_kernel,
        name="gated_embed",
        grid=(t // TM_PROJ,),
        in_specs=[
            pl.BlockSpec((TM_PROJ, D_MODEL), lambda i: (i, 0)),
            pl.BlockSpec((1, D_MODEL), lambda i: (0, 0)),
            pl.BlockSpec(w_gate.shape, lambda i: (0, 0), pipeline_mode=pl.Buffered(1)),
            pl.BlockSpec((None, TM_PROJ, PLE_DIM), lambda i: (layer, i, 0)),
            pl.BlockSpec(w_proj.shape, lambda i: (0, 0), pipeline_mode=pl.Buffered(1)),
            pl.BlockSpec((1, D_MODEL), lambda i: (0, 0)),
        ],
        out_specs=pl.BlockSpec((TM_PROJ, D_MODEL), lambda i: (i, 0)),
        out_shape=jax.ShapeDtypeStruct((t, D_MODEL), F32),
        scratch_shapes=[pltpu.VMEM((TM_PROJ, D_MODEL), BF16)],
        compiler_params=_params("parallel"),
    )(h, g_pre, w_gate, p, w_proj, g_post)


def _rope_tables(seq, rot_dim):
    inv = ROPE_THETA ** (-jnp.arange(0, rot_dim, 2, dtype=F32) / rot_dim)
    ang = jnp.arange(seq, dtype=F32)[:, None] * inv[None, :]
    return jnp.cos(ang), jnp.sin(ang)


def _lane_tables(cos, sin, tail):
    seq, half = cos.shape
    rest = LANES - 2 * half
    zeros = lambda n: jnp.zeros((seq, n), F32)
    cos_t = jnp.concatenate([cos, cos, jnp.full((seq, rest), tail, F32)], axis=1)
    sin_up = jnp.concatenate([zeros(half), sin, zeros(rest)], axis=1)
    sin_dn = jnp.concatenate([-sin, zeros(LANES - half)], axis=1)
    return cos_t, sin_up, sin_dn


def _row(v):
    return v.reshape(1, -1)


def kernel(x, p, ffn1_g_pre, ffn1_w_in, ffn1_w_out, ffn1_g_post, mix_g_pre, mix_g_post, ab_w_in, a_rel_bias, b_g_q, b_w_qup, b_g_kv, b_w_kvup, ab_w_out, c_w_in, c_lq1, c_lk1, c_lq2, c_lk2, c_g_sub, c_w_out, ffn2_g_pre, ffn2_w_in, ffn2_w_out, ffn2_g_post, ple_g_pre, ple_w_gate, ple_w_proj, ple_g_post):
    batch, seq, _ = x.shape
    t = batch * seq
    assert seq % TQ == 0 and seq % TM_IN == 0 and seq % TM_PROJ == 0 and t % TM_FFN == 0

    tab_b = _lane_tables(*_rope_tables(seq, B_ROPE_DIM), tail=0.0)
    tab_c = _lane_tables(*_rope_tables(seq, C_ROT_DIM), tail=1.0)

    ffn_f32 = [(w_in, w_out, i) for i in range(DEPTH)
               for w_in, w_out in ((ffn1_w_in, ffn1_w_out), (ffn2_w_in, ffn2_w_out))]
    ffn_f32.append(None)
    ffn_w = (ffn1_w_in[0].astype(BF16), ffn1_w_out[0].astype(BF16))
    ab_w_in, b_w_qup, b_w_kvup = (w.astype(BF16) for w in (ab_w_in, b_w_qup, b_w_kvup))
    pad_rope = LANES - B_ROPE_DIM
    w_b = jnp.pad(ab_w_in[:, :, 3 * A_WIDTH:], ((0, 0), (0, 0), (0, pad_rope)))
    w_q = b_w_qup.reshape(-1, B_Q_LORA, HEADS, B_NOPE_DIM + B_ROPE_DIM)
    w_q = jnp.pad(w_q, ((0, 0), (0, 0), (0, 0), (0, pad_rope))).reshape(-1, B_Q_LORA, HEADS * 2 * LANES)

    plain = (False,) * 8
    c_groups = (True,) * 4 + (False,) * 2

    h = x.reshape(t, D_MODEL)
    p = p.reshape(DEPTH, t, PLE_DIM)
    for i in range(DEPTH):
        j = i // 2
        h, *ffn_w = _ffn(h, _row(ffn1_g_pre[i]), *ffn_w, _row(ffn1_g_post[i]), ffn_f32[2 * i + 1])
        g_mix = _row(mix_g_pre[i])
        if i % 2 == 0:
            u_a = _proj(h, g_mix, ab_w_in, j, 3 * A_WIDTH, tab_b, seq, plain, 0)
            q_b, k_b, v_b = _mla_proj(h, g_mix, w_b[j], _row(b_g_q[j]), w_q[j], _row(b_g_kv[j]),
                                      b_w_kvup[j], tab_b, seq)
            o_a, w_out, w_gate, w_emb = _band_attn(u_a, a_rel_bias[j], batch, seq,
                                                   [(ab_w_out, j), (ple_w_gate, i), (ple_w_proj, i)])
            o_b, w_c_in = _mla_attn(q_b, k_b, v_b, batch, seq, [(c_w_in, j)])
            h = _out_proj([o_a, o_b], w_out, _row(mix_g_post[i]), h)
        else:
            lambda_init = 0.8 - 0.6 * math.exp(-0.3 * i)
            u_c = _proj(h, g_mix, w_c_in, None, HEADS * C_HEAD_IN, tab_c, seq, c_groups, C_ROT_DIM // 2)
            o_c, w_out, w_gate, w_emb = _diff_attn(
                u_c, _row(c_lq1[j]), _row(c_lk1[j]), _row(c_lq2[j]), _row(c_lk2[j]), _row(c_g_sub[j]),
                lambda_init, batch, seq, [(c_w_out, j), (ple_w_gate, i), (ple_w_proj, i)])
            h = _out_proj([o_c], w_out, _row(mix_g_post[i]), h)
        h, *ffn_w = _ffn(h, _row(ffn2_g_pre[i]), *ffn_w, _row(ffn2_g_post[i]), ffn_f32[2 * i + 2])
        h = _ple(h, _row(ple_g_pre[i]), w_gate, p, w_emb, _row(ple_g_post[i]), i)
    return h.reshape(batch, seq, D_MODEL)
```

```python
import functools
import math

import jax
import jax.numpy as jnp
from jax import lax
from jax.experimental import pallas as pl
from jax.experimental.pallas import tpu as pltpu

D_MODEL = 2048
DEPTH = 4
CHUNK = 64
N_PREV_CHUNKS = 8
ROPE_THETA = 500000.0
PLE_DIM = 256
D_FF = 5632
NORM_EPS = 1e-6
NEG_INF = -1e30
LOG2E = math.log2(math.e)

HEADS = 8
HEAD_DIM = 128
A_WIDTH = HEADS * HEAD_DIM
REL_FUTURE = CHUNK - 1
REL_PAST_CLIP = 128
REL_TABLE = REL_FUTURE + REL_PAST_CLIP + 1

B_NOPE_DIM = 128
B_ROPE_DIM = 64
B_Q_LORA = 512
B_KV_LORA = 256

C_V_DIM = 2 * HEAD_DIM
C_ROT_DIM = HEAD_DIM // 4
C_HEAD_IN = 4 * HEAD_DIM + C_V_DIM

LANES = 128
VMEM_LIMIT = 58 * 1024 * 1024

FFN_VMEM_LIMIT = 62 * 1024 * 1024

TM_FFN = 1024
TF_FFN = 512
FFN_ROW_CHUNKS = 4
TM_IN = 1024
IN_ROW_CHUNKS = 4
PROJ_ROW_CHUNKS = 2
TM_PROJ = 512
TQ = 256
A_WIN = N_PREV_CHUNKS * CHUNK + TQ
BIAS_ROW = 1024
CAST_ROW_SLABS = 8

BF16 = jnp.bfloat16
F32 = jnp.float32


def _rms(x, g):
    return x * lax.rsqrt(jnp.mean(x * x, axis=-1, keepdims=True) + NORM_EPS) * g


def _dot(a, b):
    return jnp.dot(a, b, preferred_element_type=F32)


def _dot_t(a, b):
    return lax.dot_general(a, b, (((1,), (1,)), ((), ())), preferred_element_type=F32)


def _rope_group(u, cos, sin_up, sin_dn, half):
    return (u * cos + pltpu.roll(u, half, 1) * sin_up
            + pltpu.roll(u, LANES - half, 1) * sin_dn)


def _row_chunks(rows, n):
    return [slice(r, r + rows // n) for r in range(0, rows, rows // n)]


def _pipelined(chunks, pre, mm, epi):
    n = len(chunks)
    for rows in chunks[:2]:
        pre(rows)
    nxt = mm(chunks[0])
    for c, rows in enumerate(chunks):
        cur = nxt
        if c + 2 < n:
            pre(chunks[c + 2])
        if c + 1 < n:
            nxt = mm(chunks[c + 1])
        epi(rows, cur)


def _params(*sem):
    return pltpu.CompilerParams(dimension_semantics=sem, vmem_limit_bytes=VMEM_LIMIT)


def _ffn_kernel(*refs, cast_next):
    if cast_next:
        (h_ref, gpre_ref, wa_ref, wb_ref, wo_ref, gpost_ref, nin_ref, nout_ref,
         o_ref, nin_bf_ref, nout_bf_ref, hn_ref) = refs
        nin_bf_ref[...] = nin_ref[...].astype(BF16)
        nout_bf_ref[...] = nout_ref[...].astype(BF16)
    else:
        h_ref, gpre_ref, wa_ref, wb_ref, wo_ref, gpost_ref, o_ref, hn_ref = refs
    j = pl.program_id(1)
    last = pl.num_programs(1) - 1
    chunks = _row_chunks(TM_FFN, FFN_ROW_CHUNKS)

    def swiglu(rows):
        hn = hn_ref[rows, :]
        a = _dot(hn, wa_ref[...])
        b = _dot(hn, wb_ref[...])
        return _dot((a * jax.nn.sigmoid(a) * b).astype(BF16), wo_ref[...])

    def pre_norm(rows):
        hn_ref[rows, :] = _rms(h_ref[rows, :], gpre_ref[...]).astype(BF16)

    def first(rows, part):
        o_ref[rows, :] = part

    def finish(rows, part):
        o_ref[rows, :] = h_ref[rows, :] + 0.5 * _rms(o_ref[rows, :] + part, gpost_ref[...])

    @pl.when(j == 0)
    def _():
        _pipelined(chunks, pre_norm, swiglu, first)

    @pl.when((j > 0) & (j < last))
    def _():
        o_ref[...] += swiglu(slice(None))

    @pl.when(j == last)
    def _():
        _pipelined(chunks, lambda rows: None, swiglu, finish)


def _ffn(h, g_pre, w_in, w_out, g_post, cast_next=None):
    t = h.shape[0]
    ni, nf = t // TM_FFN, D_FF // TF_FFN
    assert nf >= 3
    in_specs = [
        pl.BlockSpec((TM_FFN, D_MODEL), lambda i, j: (i, 0)),
        pl.BlockSpec((1, D_MODEL), lambda i, j: (0, 0)),
        pl.BlockSpec((D_MODEL, TF_FFN), lambda i, j: (0, j)),
        pl.BlockSpec((D_MODEL, TF_FFN), lambda i, j: (0, j + nf)),
        pl.BlockSpec((TF_FFN, D_MODEL), lambda i, j: (j, 0)),
        pl.BlockSpec((1, D_MODEL), lambda i, j: (0, 0)),
    ]
    out_specs = [pl.BlockSpec((TM_FFN, D_MODEL), lambda i, j: (i, 0))]
    out_shape = [jax.ShapeDtypeStruct((t, D_MODEL), F32)]
    args = [h, g_pre, w_in, w_in, w_out, g_post]
    if cast_next is not None:
        nxt_in, nxt_out, layer = cast_next
        in_tile = (D_MODEL // ni, 2 * D_FF // nf)
        out_tile = (D_FF // nf, D_MODEL // ni)
        in_specs += [pl.BlockSpec((None,) + in_tile, lambda i, j: (layer, i, j)),
                     pl.BlockSpec((None,) + out_tile, lambda i, j: (layer, j, i))]
        out_specs += [pl.BlockSpec(in_tile, lambda i, j: (i, j)),
                      pl.BlockSpec(out_tile, lambda i, j: (j, i))]
        out_shape += [jax.ShapeDtypeStruct(nxt_in.shape[1:], BF16),
                      jax.ShapeDtypeStruct(nxt_out.shape[1:], BF16)]
        args += [nxt_in, nxt_out]
    return pl.pallas_call(
        functools.partial(_ffn_kernel, cast_next=cast_next is not None),
        name="ffn",
        grid=(ni, nf),
        in_specs=in_specs,
        out_specs=out_specs,
        out_shape=out_shape,
        scratch_shapes=[pltpu.VMEM((TM_FFN, D_MODEL), BF16)],
        compiler_params=pltpu.CompilerParams(dimension_semantics=("parallel", "arbitrary"),
                                             vmem_limit_bytes=FFN_VMEM_LIMIT),
    )(*args)


def _proj_kernel(h_ref, g_ref, w_ref, cos_ref, sup_ref, sdn_ref, o_ref, hn_ref, *, rope_groups, half):
    chunks = _row_chunks(TM_IN, IN_ROW_CHUNKS)

    def pre_norm(rows):
        hn_ref[rows, :] = _rms(h_ref[rows, :], g_ref[...]).astype(BF16)

    def project(rows):
        return _dot(hn_ref[rows, :], w_ref[...])

    def emit(rows, u):
        if not any(rope_groups):
            o_ref[rows, :] = u.astype(o_ref.dtype)
            return
        cos, sup, sdn = cos_ref[rows, :], sup_ref[rows, :], sdn_ref[rows, :]
        for c, rotate in enumerate(rope_groups):
            sl = slice(c * LANES, (c + 1) * LANES)
            x = _rope_group(u[:, sl], cos, sup, sdn, half) if rotate else u[:, sl]
            o_ref[rows, sl] = x.astype(o_ref.dtype)

    @pl.when(pl.program_id(1) == 0)
    def _():
        _pipelined(chunks, pre_norm, project, emit)

    @pl.when(pl.program_id(1) > 0)
    def _():
        _pipelined(chunks, lambda rows: None, project, emit)


def _proj(h, g, w, n_cols, tables, seq, rope_groups, half):
    t = h.shape[0]
    tn = len(rope_groups) * LANES
    blocks_per_seq = seq // TM_IN
    tab_spec = pl.BlockSpec((TM_IN, LANES), lambda i, j: (i % blocks_per_seq, 0))
    return pl.pallas_call(
        functools.partial(_proj_kernel, rope_groups=rope_groups, half=half),
        name="in_proj",
        grid=(t // TM_IN, n_cols // tn),
        in_specs=[
            pl.BlockSpec((TM_IN, D_MODEL), lambda i, j: (i, 0)),
            pl.BlockSpec((1, D_MODEL), lambda i, j: (0, 0)),
            pl.BlockSpec((D_MODEL, tn), lambda i, j: (0, j)),
            tab_spec, tab_spec, tab_spec,
        ],
        out_specs=pl.BlockSpec((TM_IN, tn), lambda i, j: (i, j)),
        out_shape=jax.ShapeDtypeStruct((t, n_cols), BF16),
        scratch_shapes=[pltpu.VMEM((TM_IN, D_MODEL), BF16)],
        compiler_params=_params("parallel", "arbitrary"),
    )(h, g, w, *tables)


def _mla_proj_kernel(h_ref, g_ref, wb_ref, gq_ref, wq_ref, gkv_ref, wkv_ref,
                     cos_ref, sup_ref, sdn_ref, q_ref, k_ref, v_ref):
    half = B_ROPE_DIM // 2

    def project(rows):
        hn = _rms(h_ref[rows, :], g_ref[...]).astype(BF16)
        u = _dot(hn, wb_ref[...])
        cq = u[:, :B_Q_LORA]
        ckv = u[:, B_Q_LORA:B_Q_LORA + B_KV_LORA]
        q = _dot(_rms(cq, gq_ref[...]).astype(BF16), wq_ref[...])
        kv = _dot(_rms(ckv, gkv_ref[...]).astype(BF16), wkv_ref[...])
        return q, kv, u[:, B_Q_LORA + B_KV_LORA:]

    def emit(rows, q_kv_kr):
        q, kv, kr = q_kv_kr
        cos, sup, sdn = cos_ref[rows, :], sup_ref[rows, :], sdn_ref[rows, :]
        k_rope = _rope_group(kr, cos, sup, sdn, half).astype(BF16)
        for hd in range(HEADS):
            lo, mid, hi = 2 * hd * LANES, (2 * hd + 1) * LANES, (2 * hd + 2) * LANES
            q_ref[rows, lo:mid] = q[:, lo:mid].astype(BF16)
            q_ref[rows, mid:hi] = _rope_group(q[:, mid:hi], cos, sup, sdn, half).astype(BF16)
            k_ref[rows, lo:mid] = kv[:, lo:mid].astype(BF16)
            k_ref[rows, mid:hi] = k_rope
            v_ref[rows, hd * LANES:(hd + 1) * LANES] = kv[:, mid:hi].astype(BF16)

    _pipelined(_row_chunks(TM_PROJ, 2), lambda rows: None, project, emit)


def _mla_proj(h, g, wb, gq, wq, gkv, wkv, tables, seq):
    t = h.shape[0]
    blocks_per_seq = seq // TM_PROJ
    tab_spec = pl.BlockSpec((TM_PROJ, LANES), lambda i: (i % blocks_per_seq, 0))
    full = lambda a: pl.BlockSpec(a.shape, lambda i: (0, 0))
    wide = HEADS * 2 * LANES
    return pl.pallas_call(
        _mla_proj_kernel,
        name="mla_proj",
        grid=(t // TM_PROJ,),
        in_specs=[pl.BlockSpec((TM_PROJ, D_MODEL), lambda i: (i, 0)),
                  full(g), full(wb), full(gq), full(wq), full(gkv), full(wkv),
                  tab_spec, tab_spec, tab_spec],
        out_specs=[pl.BlockSpec((TM_PROJ, wide), lambda i: (i, 0)),
                   pl.BlockSpec((TM_PROJ, wide), lambda i: (i, 0)),
                   pl.BlockSpec((TM_PROJ, A_WIDTH), lambda i: (i, 0))],
        out_shape=[jax.ShapeDtypeStruct((t, wide), BF16),
                   jax.ShapeDtypeStruct((t, wide), BF16),
                   jax.ShapeDtypeStruct((t, A_WIDTH), BF16)],
        compiler_params=_params("parallel"),
    )(h, g, wb, gq, wq, gkv, wkv, *tables)


def _chunk_of(pos):
    return jnp.right_shift(pos, CHUNK.bit_length() - 1)


def _chunk_causal_mask():
    qc = _chunk_of(lax.broadcasted_iota(jnp.int32, (TQ, TQ), 0))
    kc = _chunk_of(lax.broadcasted_iota(jnp.int32, (TQ, TQ), 1))
    return kc <= qc


def _causal_weights(s, qs, mask, c):
    s_diag = jnp.where(mask, s[:, qs:], NEG_INF)
    m = jnp.max(s_diag, axis=-1, keepdims=True)
    if qs == 0:
        return [jnp.exp2((s_diag - m) * c)]
    s_off = s[:, :qs]
    m = jnp.maximum(m, jnp.max(s_off, axis=-1, keepdims=True))
    return [jnp.exp2((s_off - m) * c), jnp.exp2((s_diag - m) * c)]


def _as_bf16_row(parts):
    parts = [x.astype(BF16) for x in parts]
    return parts[0] if len(parts) == 1 else jnp.concatenate(parts, axis=1)


def _fill_value_and_ones(vaug_ref, v_ref):
    vaug_ref[:, :LANES] = v_ref[...]
    vaug_ref[:, LANES:] = jnp.ones((v_ref.shape[0], LANES), BF16)


def _grid_casts(casts, batch):
    in_specs, out_specs, out_shape = [], [], []
    for src, layer in casts:
        rows, cols = src.shape[1:]
        if cols % (HEADS * LANES) == 0:
            tile, index = (rows // batch, cols // HEADS), lambda b, h: (b, h)
        else:
            tile, index = (rows // (batch * HEADS), cols), lambda b, h: (b * HEADS + h, 0)
        in_specs.append(pl.BlockSpec((None,) + tile,
                                     lambda b, h, layer=layer, index=index: (layer,) + index(b, h)))
        out_specs.append(pl.BlockSpec(tile, index))
        out_shape.append(jax.ShapeDtypeStruct((rows, cols), BF16))
    return in_specs, out_specs, out_shape, [src for src, _ in casts]


def _cast_kernel(src_ref, dst_ref):
    dst_ref[...] = src_ref[...].astype(BF16)


def _cast_layer(src, layer):
    rows, cols = src.shape[1:]
    tile = (rows // CAST_ROW_SLABS, cols)
    return pl.pallas_call(
        _cast_kernel,
        name="cast_weights",
        grid=(CAST_ROW_SLABS,),
        in_specs=[pl.BlockSpec((None,) + tile, lambda i: (layer, i, 0))],
        out_specs=pl.BlockSpec(tile, lambda i: (i, 0)),
        out_shape=jax.ShapeDtypeStruct((rows, cols), BF16),
        compiler_params=_params("parallel"),
    )(src)


def _cast_tiles(src_refs, dst_refs):
    for src, dst in zip(src_refs, dst_refs):
        dst[...] = src[...].astype(BF16)


def _mla_attn_kernel(*refs, seq, scale, n_cast):
    q_ref, k_ref, v_ref = refs[:3]
    o_ref, vaug_ref = refs[3 + n_cast], refs[-1]
    _cast_tiles(refs[3:3 + n_cast], refs[4 + n_cast:-1])
    mask = _chunk_causal_mask()
    _fill_value_and_ones(vaug_ref, v_ref)
    scores = lambda qs: _dot_t(q_ref[qs:qs + TQ, :], k_ref[0:qs + TQ, :])
    s_next = scores(0)
    for qs in range(0, seq, TQ):
        qe = qs + TQ
        s, s_next = s_next, (scores(qe) if qe < seq else None)
        p = _as_bf16_row(_causal_weights(s, qs, mask, scale * LOG2E))
        o = _dot(p, vaug_ref[0:qe, :])
        o_ref[qs:qe, :] = (o[:, :LANES] / o[:, LANES:]).astype(o_ref.dtype)


def _mla_attn(q, k, v, batch, seq, casts):
    t = q.shape[0]
    scale = (B_NOPE_DIM + B_ROPE_DIM) ** -0.5
    cast_in, cast_out, cast_shape, cast_args = _grid_casts(casts, batch)
    return pl.pallas_call(
        functools.partial(_mla_attn_kernel, seq=seq, scale=scale, n_cast=len(casts)),
        name="mla_attn",
        grid=(batch, HEADS),
        in_specs=[pl.BlockSpec((seq, 2 * LANES), lambda b, h: (b, h)),
                  pl.BlockSpec((seq, 2 * LANES), lambda b, h: (b, h)),
                  pl.BlockSpec((seq, LANES), lambda b, h: (b, h))] + cast_in,
        out_specs=[pl.BlockSpec((seq, LANES), lambda b, h: (b, h))] + cast_out,
        out_shape=[jax.ShapeDtypeStruct((t, A_WIDTH), BF16)] + cast_shape,
        scratch_shapes=[pltpu.VMEM((seq, 2 * LANES), BF16)],
        compiler_params=_params("parallel", "parallel"),
    )(q, k, v, *cast_args)


def _diff_attn_kernel(*refs, seq, scale, lambda_init, n_cast):
    q1_ref, q2_ref, k1_ref, k2_ref, v_ref, lq1_ref, lk1_ref, lq2_ref, lk2_ref, gsub_ref = refs[:10]
    o_ref = refs[10 + n_cast]
    _cast_tiles(refs[10:10 + n_cast], refs[11 + n_cast:])
    mask = _chunk_causal_mask()
    lam = (jnp.exp(jnp.sum(lq1_ref[...] * lk1_ref[...], axis=-1, keepdims=True))
           - jnp.exp(jnp.sum(lq2_ref[...] * lk2_ref[...], axis=-1, keepdims=True)) + lambda_init)
    c = scale * LOG2E
    scores = lambda qs: (_dot_t(q1_ref[qs:qs + TQ, :], k1_ref[0:qs + TQ, :]),
                         _dot_t(q2_ref[qs:qs + TQ, :], k2_ref[0:qs + TQ, :]))
    s_next = scores(0)
    for qs in range(0, seq, TQ):
        qe = qs + TQ
        (s1, s2), s_next = s_next, (scores(qe) if qe < seq else None)
        w1 = _causal_weights(s1, qs, mask, c)
        w2 = _causal_weights(s2, qs, mask, c)
        l1 = sum(jnp.sum(x, axis=-1, keepdims=True) for x in w1)
        l2 = sum(jnp.sum(x, axis=-1, keepdims=True) for x in w2)
        o12 = _dot(jnp.concatenate([_as_bf16_row(w1), _as_bf16_row(w2)], axis=0), v_ref[0:qe, :])
        o = o12[:TQ] * (1.0 / l1) - o12[TQ:] * (lam / l2)
        o_ref[qs:qe, :] = (_rms(o, gsub_ref[...]) * (1.0 - lambda_init)).astype(o_ref.dtype)


def _diff_attn(u, lq1, lk1, lq2, lk2, g_sub, lambda_init, batch, seq, casts):
    t = u.shape[0]
    groups = C_HEAD_IN // LANES
    part = lambda c: pl.BlockSpec((seq, LANES), lambda b, h: (b, groups * h + c))
    vec = pl.BlockSpec((1, LANES), lambda b, h: (0, 0))
    cast_in, cast_out, cast_shape, cast_args = _grid_casts(casts, batch)
    return pl.pallas_call(
        functools.partial(_diff_attn_kernel, seq=seq, scale=HEAD_DIM ** -0.5, lambda_init=lambda_init,
                          n_cast=len(casts)),
        name="diff_attn",
        grid=(batch, HEADS),
        in_specs=[part(0), part(1), part(2), part(3),
                  pl.BlockSpec((seq, C_V_DIM), lambda b, h: (b, (C_HEAD_IN // C_V_DIM) * h + 2)),
                  vec, vec, vec, vec,
                  pl.BlockSpec((1, C_V_DIM), lambda b, h: (0, 0))] + cast_in,
        out_specs=[pl.BlockSpec((seq, C_V_DIM), lambda b, h: (b, h))] + cast_out,
        out_shape=[jax.ShapeDtypeStruct((t, HEADS * C_V_DIM), BF16)] + cast_shape,
        compiler_params=_params("parallel", "parallel"),
    )(u, u, u, u, u, lq1, lk1, lq2, lk2, g_sub, *cast_args)


def _band_bias_kernel(tab_ref, o_ref, *, inv_scale):
    hd = pl.program_id(0)
    first = N_PREV_CHUNKS * CHUNK
    kj = lax.broadcasted_iota(jnp.int32, (8, BIAS_ROW), 1)
    dist = jnp.where(kj < A_WIN, first - kj, first + BIAS_ROW - kj)
    idx = jnp.clip(dist, -REL_FUTURE, REL_PAST_CLIP) + REL_FUTURE

    def body(tt, acc):
        return jnp.where(idx == tt, tab_ref[hd, tt], acc)

    base = lax.fori_loop(0, REL_TABLE, body, jnp.zeros((8, BIAS_ROW), F32), unroll=8)
    sub = lax.broadcasted_iota(jnp.int32, (8, BIAS_ROW), 0)
    for bit in range(3):
        rotated = pltpu.roll(base, 1 << bit, 1)
        base = jnp.where((jnp.right_shift(sub, bit) & 1) == 1, rotated, base)
    bias = jnp.concatenate([pltpu.roll(base, 8 * b, 1) if b else base for b in range(TQ // 8)], axis=0)
    qc = _chunk_of(lax.broadcasted_iota(jnp.int32, (TQ, A_WIN), 0))
    kc = _chunk_of(lax.broadcasted_iota(jnp.int32, (TQ, A_WIN), 1))
    valid = (kc >= qc) & (kc <= qc + N_PREV_CHUNKS)
    o_ref[0] = jnp.where(valid, bias[:, :A_WIN] * inv_scale, NEG_INF)


def _band_bias(rel_table, scale):
    assert BIAS_ROW >= A_WIN + TQ
    return pl.pallas_call(
        functools.partial(_band_bias_kernel, inv_scale=1.0 / scale),
        name="band_bias",
        grid=(HEADS,),
        in_specs=[pl.BlockSpec(memory_space=pltpu.SMEM)],
        out_specs=pl.BlockSpec((1, TQ, A_WIN), lambda h: (h, 0, 0)),
        out_shape=jax.ShapeDtypeStruct((HEADS, TQ, A_WIN), F32),
        compiler_params=_params("parallel"),
    )(rel_table)


def _band_attn_kernel(*refs, seq, scale, n_cast):
    q_ref, k_ref, v_ref, bias_ref = refs[:4]
    o_ref, vaug_ref = refs[4 + n_cast], refs[-1]
    _cast_tiles(refs[4:4 + n_cast], refs[5 + n_cast:-1])
    _fill_value_and_ones(vaug_ref, v_ref)
    window = lambda qs: max(0, qs - N_PREV_CHUNKS * CHUNK)
    scores = lambda qs: _dot_t(q_ref[qs:qs + TQ, :], k_ref[window(qs):qs + TQ, :])
    s_next = scores(0)
    for qs in range(0, seq, TQ):
        qe, ws = qs + TQ, window(qs)
        s, s_next = s_next, (scores(qe) if qe < seq else None)
        s = s + bias_ref[0, :, A_WIN - (qe - ws):]
        p = jnp.exp2((s - jnp.max(s, axis=-1, keepdims=True)) * (scale * LOG2E))
        o = _dot(p.astype(BF16), vaug_ref[ws:qe, :])
        o_ref[qs:qe, :] = (o[:, :LANES] / o[:, LANES:]).astype(o_ref.dtype)


def _band_attn(u, rel_table, batch, seq, casts):
    t = u.shape[0]
    scale = HEAD_DIM ** -0.5
    part = lambda c: pl.BlockSpec((seq, LANES), lambda b, h: (b, c * HEADS + h))
    cast_in, cast_out, cast_shape, cast_args = _grid_casts(casts, batch)
    return pl.pallas_call(
        functools.partial(_band_attn_kernel, seq=seq, scale=scale, n_cast=len(casts)),
        name="band_attn",
        grid=(batch, HEADS),
        in_specs=[part(0), part(1), part(2),
                  pl.BlockSpec((1, TQ, A_WIN), lambda b, h: (h, 0, 0))] + cast_in,
        out_specs=[pl.BlockSpec((seq, LANES), lambda b, h: (b, h))] + cast_out,
        out_shape=[jax.ShapeDtypeStruct((t, A_WIDTH), BF16)] + cast_shape,
        scratch_shapes=[pltpu.VMEM((seq, 2 * LANES), BF16)],
        compiler_params=_params("parallel", "parallel"),
    )(u, u, u, _band_bias(rel_table, scale), *cast_args)


def _out_proj_kernel(*refs, n_in):
    x_refs, (w_ref, g_ref, h_ref, o_ref) = refs[:n_in], refs[n_in:]

    def project(rows):
        mix = None
        k0 = 0
        for x_ref in x_refs:
            k1 = k0 + x_ref.shape[1]
            part = _dot(x_ref[rows, :], w_ref[k0:k1, :])
            mix = part if mix is None else mix + part
            k0 = k1
        return mix

    def finish(rows, mix):
        o_ref[rows, :] = h_ref[rows, :] + _rms(mix, g_ref[...])

    _pipelined(_row_chunks(TM_PROJ, PROJ_ROW_CHUNKS), lambda rows: None, project, finish)


def _out_proj(xs, w, g, h):
    t = h.shape[0]
    return pl.pallas_call(
        functools.partial(_out_proj_kernel, n_in=len(xs)),
        name="out_proj",
        grid=(t // TM_PROJ,),
        in_specs=[pl.BlockSpec((TM_PROJ, x.shape[1]), lambda i: (i, 0)) for x in xs] + [
            pl.BlockSpec(w.shape, lambda i: (0, 0), pipeline_mode=pl.Buffered(1)),
            pl.BlockSpec((1, D_MODEL), lambda i: (0, 0)),
            pl.BlockSpec((TM_PROJ, D_MODEL), lambda i: (i, 0)),
        ],
        out_specs=pl.BlockSpec((TM_PROJ, D_MODEL), lambda i: (i, 0)),
        out_shape=jax.ShapeDtypeStruct((t, D_MODEL), F32),
        compiler_params=_params("parallel"),
    )(*xs, w, g, h)


def _ple_kernel(h_ref, gpre_ref, wg_ref, p_ref, wp_ref, gpost_ref, o_ref, hn_ref):
    def pre_norm(rows):
        hn_ref[rows, :] = _rms(h_ref[rows, :], gpre_ref[...]).astype(BF16)

    def project(rows):
        return _dot(hn_ref[rows, :], wg_ref[...]), _dot(p_ref[rows, :].astype(BF16), wp_ref[...])

    def finish(rows, gate_emb):
        gate, emb = gate_emb
        o_ref[rows, :] = h_ref[rows, :] + _rms(jax.nn.sigmoid(gate) * emb, gpost_ref[...])

    _pipelined(_row_chunks(TM_PROJ, PROJ_ROW_CHUNKS), pre_norm, project, finish)


def _ple(h, g_pre, w_gate, p, w_proj, g_post, layer):
    t = h.shape[0]
    return pl.pallas_call(
        _ple_kernel,
        name="gated_embed",
        grid=(t // TM_PROJ,),
        in_specs=[
            pl.BlockSpec((TM_PROJ, D_MODEL), lambda i: (i, 0)),
            pl.BlockSpec((1, D_MODEL), lambda i: (0, 0)),
            pl.BlockSpec(w_gate.shape, lambda i: (0, 0), pipeline_mode=pl.Buffered(1)),
            pl.BlockSpec((None, TM_PROJ, PLE_DIM), lambda i: (layer, i, 0)),
            pl.BlockSpec(w_proj.shape, lambda i: (0, 0), pipeline_mode=pl.Buffered(1)),
            pl.BlockSpec((1, D_MODEL), lambda i: (0, 0)),
        ],
        out_specs=pl.BlockSpec((TM_PROJ, D_MODEL), lambda i: (i, 0)),
        out_shape=jax.ShapeDtypeStruct((t, D_MODEL), F32),
        scratch_shapes=[pltpu.VMEM((TM_PROJ, D_MODEL), BF16)],
        compiler_params=_params("parallel"),
    )(h, g_pre, w_gate, p, w_proj, g_post)


def _rope_tables(seq, rot_dim):
    inv = ROPE_THETA ** (-jnp.arange(0, rot_dim, 2, dtype=F32) / rot_dim)
    ang = jnp.arange(seq, dtype=F32)[:, None] * inv[None, :]
    return jnp.cos(ang), jnp.sin(ang)


def _lane_tables(cos, sin, tail):
    seq, half = cos.shape
    rest = LANES - 2 * half
    zeros = lambda n: jnp.zeros((seq, n), F32)
    cos_t = jnp.concatenate([cos, cos, jnp.full((seq, rest), tail, F32)], axis=1)
    sin_up = jnp.concatenate([zeros(half), sin, zeros(rest)], axis=1)
    sin_dn = jnp.concatenate([-sin, zeros(LANES - half)], axis=1)
    return cos_t, sin_up, sin_dn


def _row(v):
    return v.reshape(1, -1)


def kernel(x, p, ffn1_g_pre, ffn1_w_in, ffn1_w_out, ffn1_g_post, mix_g_pre, mix_g_post, ab_w_in, a_rel_bias, b_g_q, b_w_qup, b_g_kv, b_w_kvup, ab_w_out, c_w_in, c_lq1, c_lk1, c_lq2, c_lk2, c_g_sub, c_w_out, ffn2_g_pre, ffn2_w_in, ffn2_w_out, ffn2_g_post, ple_g_pre, ple_w_gate, ple_w_proj, ple_g_post):
    batch, seq, _ = x.shape
    t = batch * seq
    assert seq % TQ == 0 and seq % TM_IN == 0 and seq % TM_PROJ == 0 and t % TM_FFN == 0

    tab_b = _lane_tables(*_rope_tables(seq, B_ROPE_DIM), tail=0.0)
    tab_c = _lane_tables(*_rope_tables(seq, C_ROT_DIM), tail=1.0)

    ffn_f32 = [(w_in, w_out, i) for i in range(DEPTH)
               for w_in, w_out in ((ffn1_w_in, ffn1_w_out), (ffn2_w_in, ffn2_w_out))]
    ffn_f32.append(None)
    ffn_w = (ffn1_w_in[0].astype(BF16), ffn1_w_out[0].astype(BF16))
    w_mix_in = _cast_layer(ab_w_in, 0)
    pad_rope = LANES - B_ROPE_DIM
    w_q = b_w_qup.astype(BF16).reshape(-1, B_Q_LORA, HEADS, B_NOPE_DIM + B_ROPE_DIM)
    w_q = jnp.pad(w_q, ((0, 0), (0, 0), (0, 0), (0, pad_rope))).reshape(-1, B_Q_LORA, HEADS * 2 * LANES)
    b_w_kvup = b_w_kvup.astype(BF16)

    plain = (False,) * 8
    c_groups = (True,) * 4 + (False,) * 2

    h = x.reshape(t, D_MODEL)
    p = p.reshape(DEPTH, t, PLE_DIM)
    for i in range(DEPTH):
        j = i // 2
        h, *ffn_w = _ffn(h, _row(ffn1_g_pre[i]), *ffn_w, _row(ffn1_g_post[i]), ffn_f32[2 * i + 1])
        g_mix = _row(mix_g_pre[i])
        if i % 2 == 0:
            u_a = _proj(h, g_mix, w_mix_in, 3 * A_WIDTH, tab_b, seq, plain, 0)
            w_b = jnp.pad(w_mix_in[:, 3 * A_WIDTH:], ((0, 0), (0, pad_rope)))
            q_b, k_b, v_b = _mla_proj(h, g_mix, w_b, _row(b_g_q[j]), w_q[j], _row(b_g_kv[j]),
                                      b_w_kvup[j], tab_b, seq)
            o_a, w_out, w_gate, w_emb = _band_attn(u_a, a_rel_bias[j], batch, seq,
                                                   [(ab_w_out, j), (ple_w_gate, i), (ple_w_proj, i)])
            o_b, w_mix_in = _mla_attn(q_b, k_b, v_b, batch, seq, [(c_w_in, j)])
            h = _out_proj([o_a, o_b], w_out, _row(mix_g_post[i]), h)
        else:
            lambda_init = 0.8 - 0.6 * math.exp(-0.3 * i)
            u_c = _proj(h, g_mix, w_mix_in, HEADS * C_HEAD_IN, tab_c, seq, c_groups, C_ROT_DIM // 2)
            casts = [(c_w_out, j), (ple_w_gate, i), (ple_w_proj, i)]
            if i + 1 < DEPTH:
                casts.append((ab_w_in, j + 1))
            o_c, w_out, w_gate, w_emb, *w_next = _diff_attn(
                u_c, _row(c_lq1[j]), _row(c_lk1[j]), _row(c_lq2[j]), _row(c_lk2[j]), _row(c_g_sub[j]),
                lambda_init, batch, seq, casts)
            if w_next:
                w_mix_in, = w_next
            h = _out_proj([o_c], w_out, _row(mix_g_post[i]), h)
        h, *ffn_w = _ffn(h, _row(ffn2_g_pre[i]), *ffn_w, _row(ffn2_g_post[i]), ffn_f32[2 * i + 2])
        h = _ple(h, _row(ple_g_pre[i]), w_gate, p, w_emb, _row(ple_g_post[i]), i)
    return h.reshape(batch, seq, D_MODEL)
```

```python
import functools
import math

import jax
import jax.numpy as jnp
from jax import lax
from jax.experimental import pallas as pl
from jax.experimental.pallas import tpu as pltpu

D_MODEL = 2048
DEPTH = 4
CHUNK = 64
N_PREV_CHUNKS = 8
ROPE_THETA = 500000.0
PLE_DIM = 256
D_FF = 5632
NORM_EPS = 1e-6
NEG_INF = -1e30
LOG2E = math.log2(math.e)

HEADS = 8
HEAD_DIM = 128
A_WIDTH = HEADS * HEAD_DIM
REL_FUTURE = CHUNK - 1
REL_PAST_CLIP = 128
REL_TABLE = REL_FUTURE + REL_PAST_CLIP + 1

B_NOPE_DIM = 128
B_ROPE_DIM = 64
B_Q_LORA = 512
B_KV_LORA = 256

C_V_DIM = 2 * HEAD_DIM
C_ROT_DIM = HEAD_DIM // 4
C_HEAD_IN = 4 * HEAD_DIM + C_V_DIM

LANES = 128
VMEM_LIMIT = 58 * 1024 * 1024

FFN_VMEM_LIMIT = 62 * 1024 * 1024

TM_FFN = 1024
TF_FFN = 512
FFN_ROW_CHUNKS = 4
TM_IN = 1024
IN_ROW_CHUNKS = 4
PROJ_ROW_CHUNKS = 2
TM_PROJ = 512
TQ = 256
A_WIN = N_PREV_CHUNKS * CHUNK + TQ
BIAS_ROW = 1024
CAST_ROW_SLABS = 8

BF16 = jnp.bfloat16
F32 = jnp.float32


def _rms(x, g):
    return x * lax.rsqrt(jnp.mean(x * x, axis=-1, keepdims=True) + NORM_EPS) * g


def _dot(a, b):
    return jnp.dot(a, b, preferred_element_type=F32)


def _dot_t(a, b):
    return lax.dot_general(a, b, (((1,), (1,)), ((), ())), preferred_element_type=F32)


def _rope_group(u, cos, sin_up, sin_dn, half):
    return (u * cos + pltpu.roll(u, half, 1) * sin_up
            + pltpu.roll(u, LANES - half, 1) * sin_dn)


def _row_chunks(rows, n):
    return [slice(r, r + rows // n) for r in range(0, rows, rows // n)]


def _pipelined(chunks, pre, mm, epi):
    n = len(chunks)
    for rows in chunks[:2]:
        pre(rows)
    nxt = mm(chunks[0])
    for c, rows in enumerate(chunks):
        cur = nxt
        if c + 2 < n:
            pre(chunks[c + 2])
        if c + 1 < n:
            nxt = mm(chunks[c + 1])
        epi(rows, cur)


def _params(*sem):
    return pltpu.CompilerParams(dimension_semantics=sem, vmem_limit_bytes=VMEM_LIMIT)


def _ffn_kernel(*refs, cast_next):
    if cast_next:
        (h_ref, gpre_ref, wa_ref, wb_ref, wo_ref, gpost_ref, nin_ref, nout_ref,
         o_ref, nin_bf_ref, nout_bf_ref, hn_ref) = refs
        nin_bf_ref[...] = nin_ref[...].astype(BF16)
        nout_bf_ref[...] = nout_ref[...].astype(BF16)
    else:
        h_ref, gpre_ref, wa_ref, wb_ref, wo_ref, gpost_ref, o_ref, hn_ref = refs
    j = pl.program_id(1)
    last = pl.num_programs(1) - 1
    chunks = _row_chunks(TM_FFN, FFN_ROW_CHUNKS)

    def swiglu(rows):
        hn = hn_ref[rows, :]
        a = _dot(hn, wa_ref[...])
        b = _dot(hn, wb_ref[...])
        return _dot((a * jax.nn.sigmoid(a) * b).astype(BF16), wo_ref[...])

    def pre_norm(rows):
        hn_ref[rows, :] = _rms(h_ref[rows, :], gpre_ref[...]).astype(BF16)

    def first(rows, part):
        o_ref[rows, :] = part

    def finish(rows, part):
        o_ref[rows, :] = h_ref[rows, :] + 0.5 * _rms(o_ref[rows, :] + part, gpost_ref[...])

    @pl.when(j == 0)
    def _():
        _pipelined(chunks, pre_norm, swiglu, first)

    @pl.when((j > 0) & (j < last))
    def _():
        o_ref[...] += swiglu(slice(None))

    @pl.when(j == last)
    def _():
        _pipelined(chunks, lambda rows: None, swiglu, finish)


def _ffn(h, g_pre, w_in, w_out, g_post, cast_next=None):
    t = h.shape[0]
    ni, nf = t // TM_FFN, D_FF // TF_FFN
    assert nf >= 3
    in_specs = [
        pl.BlockSpec((TM_FFN, D_MODEL), lambda i, j: (i, 0)),
        pl.BlockSpec((1, D_MODEL), lambda i, j: (0, 0)),
        pl.BlockSpec((D_MODEL, TF_FFN), lambda i, j: (0, j)),
        pl.BlockSpec((D_MODEL, TF_FFN), lambda i, j: (0, j + nf)),
        pl.BlockSpec((TF_FFN, D_MODEL), lambda i, j: (j, 0)),
        pl.BlockSpec((1, D_MODEL), lambda i, j: (0, 0)),
    ]
    out_specs = [pl.BlockSpec((TM_FFN, D_MODEL), lambda i, j: (i, 0))]
    out_shape = [jax.ShapeDtypeStruct((t, D_MODEL), F32)]
    args = [h, g_pre, w_in, w_in, w_out, g_post]
    if cast_next is not None:
        nxt_in, nxt_out, layer = cast_next
        in_tile = (D_MODEL // ni, 2 * D_FF // nf)
        out_tile = (D_FF // nf, D_MODEL // ni)
        in_specs += [pl.BlockSpec((None,) + in_tile, lambda i, j: (layer, i, j)),
                     pl.BlockSpec((None,) + out_tile, lambda i, j: (layer, j, i))]
        out_specs += [pl.BlockSpec(in_tile, lambda i, j: (i, j)),
                      pl.BlockSpec(out_tile, lambda i, j: (j, i))]
        out_shape += [jax.ShapeDtypeStruct(nxt_in.shape[1:], BF16),
                      jax.ShapeDtypeStruct(nxt_out.shape[1:], BF16)]
        args += [nxt_in, nxt_out]
    return pl.pallas_call(
        functools.partial(_ffn_kernel, cast_next=cast_next is not None),
        name="ffn",
        grid=(ni, nf),
        in_specs=in_specs,
        out_specs=out_specs,
        out_shape=out_shape,
        scratch_shapes=[pltpu.VMEM((TM_FFN, D_MODEL), BF16)],
        compiler_params=pltpu.CompilerParams(dimension_semantics=("parallel", "arbitrary"),
                                             vmem_limit_bytes=FFN_VMEM_LIMIT),
    )(*args)


def _proj_kernel(h_ref, g_ref, w_ref, cos_ref, sup_ref, sdn_ref, o_ref, hn_ref, *, rope_groups, half):
    chunks = _row_chunks(TM_IN, IN_ROW_CHUNKS)

    def pre_norm(rows):
        hn_ref[rows, :] = _rms(h_ref[rows, :], g_ref[...]).astype(BF16)

    def project(rows):
        return _dot(hn_ref[rows, :], w_ref[...])

    def emit(rows, u):
        if not any(rope_groups):
            o_ref[rows, :] = u.astype(o_ref.dtype)
            return
        cos, sup, sdn = cos_ref[rows, :], sup_ref[rows, :], sdn_ref[rows, :]
        for c, rotate in enumerate(rope_groups):
            sl = slice(c * LANES, (c + 1) * LANES)
            x = _rope_group(u[:, sl], cos, sup, sdn, half) if rotate else u[:, sl]
            o_ref[rows, sl] = x.astype(o_ref.dtype)

    @pl.when(pl.program_id(1) == 0)
    def _():
        _pipelined(chunks, pre_norm, project, emit)

    @pl.when(pl.program_id(1) > 0)
    def _():
        _pipelined(chunks, lambda rows: None, project, emit)


def _proj(h, g, w, n_cols, tables, seq, rope_groups, half):
    t = h.shape[0]
    tn = len(rope_groups) * LANES
    blocks_per_seq = seq // TM_IN
    tab_spec = pl.BlockSpec((TM_IN, LANES), lambda i, j: (i % blocks_per_seq, 0))
    return pl.pallas_call(
        functools.partial(_proj_kernel, rope_groups=rope_groups, half=half),
        name="in_proj",
        grid=(t // TM_IN, n_cols // tn),
        in_specs=[
            pl.BlockSpec((TM_IN, D_MODEL), lambda i, j: (i, 0)),
            pl.BlockSpec((1, D_MODEL), lambda i, j: (0, 0)),
            pl.BlockSpec((D_MODEL, tn), lambda i, j: (0, j)),
            tab_spec, tab_spec, tab_spec,
        ],
        out_specs=pl.BlockSpec((TM_IN, tn), lambda i, j: (i, j)),
        out_shape=jax.ShapeDtypeStruct((t, n_cols), BF16),
        scratch_shapes=[pltpu.VMEM((TM_IN, D_MODEL), BF16)],
        compiler_params=_params("parallel", "arbitrary"),
    )(h, g, w, *tables)


def _mla_proj_kernel(h_ref, g_ref, wb_ref, gq_ref, wq_ref, gkv_ref, wkv_ref,
                     cos_ref, sup_ref, sdn_ref, q_ref, k_ref, v_ref):
    half = B_ROPE_DIM // 2

    def project(rows):
        hn = _rms(h_ref[rows, :], g_ref[...]).astype(BF16)
        u = _dot(hn, wb_ref[...])
        cq = u[:, :B_Q_LORA]
        ckv = u[:, B_Q_LORA:B_Q_LORA + B_KV_LORA]
        q = _dot(_rms(cq, gq_ref[...]).astype(BF16), wq_ref[...])
        kv = _dot(_rms(ckv, gkv_ref[...]).astype(BF16), wkv_ref[...])
        return q, kv, u[:, B_Q_LORA + B_KV_LORA:]

    def emit(rows, q_kv_kr):
        q, kv, kr = q_kv_kr
        cos, sup, sdn = cos_ref[rows, :], sup_ref[rows, :], sdn_ref[rows, :]
        k_rope = _rope_group(kr, cos, sup, sdn, half).astype(BF16)
        for hd in range(HEADS):
            lo, mid, hi = 2 * hd * LANES, (2 * hd + 1) * LANES, (2 * hd + 2) * LANES
            q_ref[rows, lo:mid] = q[:, lo:mid].astype(BF16)
            q_ref[rows, mid:hi] = _rope_group(q[:, mid:hi], cos, sup, sdn, half).astype(BF16)
            k_ref[rows, lo:mid] = kv[:, lo:mid].astype(BF16)
            k_ref[rows, mid:hi] = k_rope
            v_ref[rows, hd * LANES:(hd + 1) * LANES] = kv[:, mid:hi].astype(BF16)

    _pipelined(_row_chunks(TM_PROJ, 2), lambda rows: None, project, emit)


def _mla_proj(h, g, wb, gq, wq, gkv, wkv, tables, seq):
    t = h.shape[0]
    blocks_per_seq = seq // TM_PROJ
    tab_spec = pl.BlockSpec((TM_PROJ, LANES), lambda i: (i % blocks_per_seq, 0))
    full = lambda a: pl.BlockSpec(a.shape, lambda i: (0, 0))
    wide = HEADS * 2 * LANES
    return pl.pallas_call(
        _mla_proj_kernel,
        name="mla_proj",
        grid=(t // TM_PROJ,),
        in_specs=[pl.BlockSpec((TM_PROJ, D_MODEL), lambda i: (i, 0)),
                  full(g), full(wb), full(gq), full(wq), full(gkv), full(wkv),
                  tab_spec, tab_spec, tab_spec],
        out_specs=[pl.BlockSpec((TM_PROJ, wide), lambda i: (i, 0)),
                   pl.BlockSpec((TM_PROJ, wide), lambda i: (i, 0)),
                   pl.BlockSpec((TM_PROJ, A_WIDTH), lambda i: (i, 0))],
        out_shape=[jax.ShapeDtypeStruct((t, wide), BF16),
                   jax.ShapeDtypeStruct((t, wide), BF16),
                   jax.ShapeDtypeStruct((t, A_WIDTH), BF16)],
        compiler_params=_params("parallel"),
    )(h, g, wb, gq, wq, gkv, wkv, *tables)


def _chunk_of(pos):
    return jnp.right_shift(pos, CHUNK.bit_length() - 1)


def _chunk_causal_mask():
    qc = _chunk_of(lax.broadcasted_iota(jnp.int32, (TQ, TQ), 0))
    kc = _chunk_of(lax.broadcasted_iota(jnp.int32, (TQ, TQ), 1))
    return kc <= qc


def _causal_weights(s, qs, mask, c):
    s_diag = jnp.where(mask, s[:, qs:], NEG_INF)
    m = jnp.max(s_diag, axis=-1, keepdims=True)
    if qs == 0:
        return [jnp.exp2((s_diag - m) * c)]
    s_off = s[:, :qs]
    m = jnp.maximum(m, jnp.max(s_off, axis=-1, keepdims=True))
    return [jnp.exp2((s_off - m) * c), jnp.exp2((s_diag - m) * c)]


def _as_bf16_row(parts):
    parts = [x.astype(BF16) for x in parts]
    return parts[0] if len(parts) == 1 else jnp.concatenate(parts, axis=1)


def _fill_value_and_ones(vaug_ref, v_ref):
    vaug_ref[:, :LANES] = v_ref[...]
    vaug_ref[:, LANES:] = jnp.ones((v_ref.shape[0], LANES), BF16)


def _grid_casts(casts, batch):
    in_specs, out_specs, out_shape = [], [], []
    for src, layer in casts:
        rows, cols = src.shape[1:]
        if cols % (HEADS * LANES) == 0:
            tile, index = (rows // batch, cols // HEADS), lambda b, h: (b, h)
        else:
            tile, index = (rows // (batch * HEADS), cols), lambda b, h: (b * HEADS + h, 0)
        in_specs.append(pl.BlockSpec((None,) + tile,
                                     lambda b, h, layer=layer, index=index: (layer,) + index(b, h)))
        out_specs.append(pl.BlockSpec(tile, index))
        out_shape.append(jax.ShapeDtypeStruct((rows, cols), BF16))
    return in_specs, out_specs, out_shape, [src for src, _ in casts]


def _cast_kernel(src_ref, dst_ref):
    dst_ref[...] = src_ref[...].astype(BF16)


def _cast_layer(src, layer):
    rows, cols = src.shape[1:]
    tile = (rows // CAST_ROW_SLABS, cols)
    return pl.pallas_call(
        _cast_kernel,
        name="cast_weights",
        grid=(CAST_ROW_SLABS,),
        in_specs=[pl.BlockSpec((None,) + tile, lambda i: (layer, i, 0))],
        out_specs=pl.BlockSpec(tile, lambda i: (i, 0)),
        out_shape=jax.ShapeDtypeStruct((rows, cols), BF16),
        compiler_params=_params("parallel"),
    )(src)


def _cast_tiles(src_refs, dst_refs):
    for src, dst in zip(src_refs, dst_refs):
        dst[...] = src[...].astype(BF16)


def _mla_attn_kernel(*refs, seq, scale, n_cast):
    q_ref, k_ref, v_ref = refs[:3]
    o_ref, vaug_ref = refs[3 + n_cast], refs[-1]
    _cast_tiles(refs[3:3 + n_cast], refs[4 + n_cast:-1])
    mask = _chunk_causal_mask()
    _fill_value_and_ones(vaug_ref, v_ref)
    scores = lambda qs: _dot_t(q_ref[qs:qs + TQ, :], k_ref[0:qs + TQ, :])
    s_next = scores(0)
    for qs in range(0, seq, TQ):
        qe = qs + TQ
        s, s_next = s_next, (scores(qe) if qe < seq else None)
        p = _as_bf16_row(_causal_weights(s, qs, mask, scale * LOG2E))
        o = _dot(p, vaug_ref[0:qe, :])
        o_ref[qs:qe, :] = (o[:, :LANES] / o[:, LANES:]).astype(o_ref.dtype)


def _mla_attn(q, k, v, batch, seq, casts):
    t = q.shape[0]
    scale = (B_NOPE_DIM + B_ROPE_DIM) ** -0.5
    cast_in, cast_out, cast_shape, cast_args = _grid_casts(casts, batch)
    return pl.pallas_call(
        functools.partial(_mla_attn_kernel, seq=seq, scale=scale, n_cast=len(casts)),
        name="mla_attn",
        grid=(batch, HEADS),
        in_specs=[pl.BlockSpec((seq, 2 * LANES), lambda b, h: (b, h)),
                  pl.BlockSpec((seq, 2 * LANES), lambda b, h: (b, h)),
                  pl.BlockSpec((seq, LANES), lambda b, h: (b, h))] + cast_in,
        out_specs=[pl.BlockSpec((seq, LANES), lambda b, h: (b, h))] + cast_out,
        out_shape=[jax.ShapeDtypeStruct((t, A_WIDTH), BF16)] + cast_shape,
        scratch_shapes=[pltpu.VMEM((seq, 2 * LANES), BF16)],
        compiler_params=_params("parallel", "parallel"),
    )(q, k, v, *cast_args)


def _diff_attn_kernel(*refs, seq, scale, lambda_init, n_cast):
    q1_ref, q2_ref, k1_ref, k2_ref, v_ref, lq1_ref, lk1_ref, lq2_ref, lk2_ref, gsub_ref = refs[:10]
    o_ref = refs[10 + n_cast]
    _cast_tiles(refs[10:10 + n_cast], refs[11 + n_cast:])
    mask = _chunk_causal_mask()
    lam = (jnp.exp(jnp.sum(lq1_ref[...] * lk1_ref[...], axis=-1, keepdims=True))
           - jnp.exp(jnp.sum(lq2_ref[...] * lk2_ref[...], axis=-1, keepdims=True)) + lambda_init)
    c = scale * LOG2E
    scores = lambda qs: (_dot_t(q1_ref[qs:qs + TQ, :], k1_ref[0:qs + TQ, :]),
                         _dot_t(q2_ref[qs:qs + TQ, :], k2_ref[0:qs + TQ, :]))
    s_next = scores(0)
    for qs in range(0, seq, TQ):
        qe = qs + TQ
        (s1, s2), s_next = s_next, (scores(qe) if qe < seq else None)
        w1 = _causal_weights(s1, qs, mask, c)
        w2 = _causal_weights(s2, qs, mask, c)
        l1 = sum(jnp.sum(x, axis=-1, keepdims=True) for x in w1)
        l2 = sum(jnp.sum(x, axis=-1, keepdims=True) for x in w2)
        o12 = _dot(jnp.concatenate([_as_bf16_row(w1), _as_bf16_row(w2)], axis=0), v_ref[0:qe, :])
        o = o12[:TQ] * (1.0 / l1) - o12[TQ:] * (lam / l2)
        o_ref[qs:qe, :] = (_rms(o, gsub_ref[...]) * (1.0 - lambda_init)).astype(o_ref.dtype)


def _diff_attn(u, lq1, lk1, lq2, lk2, g_sub, lambda_init, batch, seq, casts):
    t = u.shape[0]
    groups = C_HEAD_IN // LANES
    part = lambda c: pl.BlockSpec((seq, LANES), lambda b, h: (b, groups * h + c))
    vec = pl.BlockSpec((1, LANES), lambda b, h: (0, 0))
    cast_in, cast_out, cast_shape, cast_args = _grid_casts(casts, batch)
    return pl.pallas_call(
        functools.partial(_diff_attn_kernel, seq=seq, scale=HEAD_DIM ** -0.5, lambda_init=lambda_init,
                          n_cast=len(casts)),
        name="diff_attn",
        grid=(batch, HEADS),
        in_specs=[part(0), part(1), part(2), part(3),
                  pl.BlockSpec((seq, C_V_DIM), lambda b, h: (b, (C_HEAD_IN // C_V_DIM) * h + 2)),
                  vec, vec, vec, vec,
                  pl.BlockSpec((1, C_V_DIM), lambda b, h: (0, 0))] + cast_in,
        out_specs=[pl.BlockSpec((seq, C_V_DIM), lambda b, h: (b, h))] + cast_out,
        out_shape=[jax.ShapeDtypeStruct((t, HEADS * C_V_DIM), BF16)] + cast_shape,
        compiler_params=_params("parallel", "parallel"),
    )(u, u, u, u, u, lq1, lk1, lq2, lk2, g_sub, *cast_args)


def _band_bias_kernel(tab_ref, o_ref, *, inv_scale):
    hd = pl.program_id(0)
    first = N_PREV_CHUNKS * CHUNK
    kj = lax.broadcasted_iota(jnp.int32, (8, BIAS_ROW), 1)
    dist = jnp.where(kj < A_WIN, first - kj, first + BIAS_ROW - kj)
    idx = jnp.clip(dist, -REL_FUTURE, REL_PAST_CLIP) + REL_FUTURE

    def body(tt, acc):
        return jnp.where(idx == tt, tab_ref[hd, tt], acc)

    base = lax.fori_loop(0, REL_TABLE, body, jnp.zeros((8, BIAS_ROW), F32), unroll=8)
    sub = lax.broadcasted_iota(jnp.int32, (8, BIAS_ROW), 0)
    for bit in range(3):
        rotated = pltpu.roll(base, 1 << bit, 1)
        base = jnp.where((jnp.right_shift(sub, bit) & 1) == 1, rotated, base)
    bias = jnp.concatenate([pltpu.roll(base, 8 * b, 1) if b else base for b in range(TQ // 8)], axis=0)
    qc = _chunk_of(lax.broadcasted_iota(jnp.int32, (TQ, A_WIN), 0))
    kc = _chunk_of(lax.broadcasted_iota(jnp.int32, (TQ, A_WIN), 1))
    valid = (kc >= qc) & (kc <= qc + N_PREV_CHUNKS)
    o_ref[0] = jnp.where(valid, bias[:, :A_WIN] * inv_scale, NEG_INF)


def _band_bias(rel_table, scale):
    assert BIAS_ROW >= A_WIN + TQ
    return pl.pallas_call(
        functools.partial(_band_bias_kernel, inv_scale=1.0 / scale),
        name="band_bias",
        grid=(HEADS,),
        in_specs=[pl.BlockSpec(memory_space=pltpu.SMEM)],
        out_specs=pl.BlockSpec((1, TQ, A_WIN), lambda h: (h, 0, 0)),
        out_shape=jax.ShapeDtypeStruct((HEADS, TQ, A_WIN), F32),
        compiler_params=_params("parallel"),
    )(rel_table)


def _band_attn_kernel(*refs, seq, scale, n_cast):
    q_ref, k_ref, v_ref, bias_ref = refs[:4]
    o_ref, vaug_ref = refs[4 + n_cast], refs[-1]
    _cast_tiles(refs[4:4 + n_cast], refs[5 + n_cast:-1])
    _fill_value_and_ones(vaug_ref, v_ref)
    window = lambda qs: max(0, qs - N_PREV_CHUNKS * CHUNK)
    scores = lambda qs: _dot_t(q_ref[qs:qs + TQ, :], k_ref[window(qs):qs + TQ, :])
    s_next = scores(0)
    for qs in range(0, seq, TQ):
        qe, ws = qs + TQ, window(qs)
        s, s_next = s_next, (scores(qe) if qe < seq else None)
        s = s + bias_ref[0, :, A_WIN - (qe - ws):]
        p = jnp.exp2((s - jnp.max(s, axis=-1, keepdims=True)) * (scale * LOG2E))
        o = _dot(p.astype(BF16), vaug_ref[ws:qe, :])
        o_ref[qs:qe, :] = (o[:, :LANES] / o[:, LANES:]).astype(o_ref.dtype)


def _band_attn(u, rel_table, batch, seq, casts):
    t = u.shape[0]
    scale = HEAD_DIM ** -0.5
    part = lambda c: pl.BlockSpec((seq, LANES), lambda b, h: (b, c * HEADS + h))
    cast_in, cast_out, cast_shape, cast_args = _grid_casts(casts, batch)
    return pl.pallas_call(
        functools.partial(_band_attn_kernel, seq=seq, scale=scale, n_cast=len(casts)),
        name="band_attn",
        grid=(batch, HEADS),
        in_specs=[part(0), part(1), part(2),
                  pl.BlockSpec((1, TQ, A_WIN), lambda b, h: (h, 0, 0))] + cast_in,
        out_specs=[pl.BlockSpec((seq, LANES), lambda b, h: (b, h))] + cast_out,
        out_shape=[jax.ShapeDtypeStruct((t, A_WIDTH), BF16)] + cast_shape,
        scratch_shapes=[pltpu.VMEM((seq, 2 * LANES), BF16)],
        compiler_params=_params("parallel", "parallel"),
    )(u, u, u, _band_bias(rel_table, scale), *cast_args)


def _out_proj_kernel(*refs, n_in):
    x_refs, (w_ref, g_ref, h_ref, o_ref) = refs[:n_in], refs[n_in:]

    def project(rows):
        mix = None
        k0 = 0
        for x_ref in x_refs:
            k1 = k0 + x_ref.shape[1]
            part = _dot(x_ref[rows, :], w_ref[k0:k1, :])
            mix = part if mix is None else mix + part
            k0 = k1
        return mix

    def finish(rows, mix):
        o_ref[rows, :] = h_ref[rows, :] + _rms(mix, g_ref[...])

    _pipelined(_row_chunks(TM_PROJ, PROJ_ROW_CHUNKS), lambda rows: None, project, finish)


def _out_proj(xs, w, g, h):
    t = h.shape[0]
    return pl.pallas_call(
        functools.partial(_out_proj_kernel, n_in=len(xs)),
        name="out_proj",
        grid=(t // TM_PROJ,),
        in_specs=[pl.BlockSpec((TM_PROJ, x.shape[1]), lambda i: (i, 0)) for x in xs] + [
            pl.BlockSpec(w.shape, lambda i: (0, 0), pipeline_mode=pl.Buffered(1)),
            pl.BlockSpec((1, D_MODEL), lambda i: (0, 0)),
            pl.BlockSpec((TM_PROJ, D_MODEL), lambda i: (i, 0)),
        ],
        out_specs=pl.BlockSpec((TM_PROJ, D_MODEL), lambda i: (i, 0)),
        out_shape=jax.ShapeDtypeStruct((t, D_MODEL), F32),
        compiler_params=_params("parallel"),
    )(*xs, w, g, h)


def _ple_kernel(h_ref, gpre_ref, wg_ref, p_ref, wp_ref, gpost_ref, o_ref, hn_ref):
    def pre_norm(rows):
        hn_ref[rows, :] = _rms(h_ref[rows, :], gpre_ref[...]).astype(BF16)

    def project(rows):
        return _dot(hn_ref[rows, :], wg_ref[...]), _dot(p_ref[rows, :].astype(BF16), wp_ref[...])

    def finish(rows, gate_emb):
        gate, emb = gate_emb
        o_ref[rows, :] = h_ref[rows, :] + _rms(jax.nn.sigmoid(gate) * emb, gpost_ref[...])

    _pipelined(_row_chunks(TM_PROJ, PROJ_ROW_CHUNKS), pre_norm, project, finish)


def _ple(h, g_pre, w_gate, p, w_proj, g_post, layer):
    t = h.shape[0]
    return pl.pallas_call(
        _ple_kernel,
        name="gated_embed",
        grid=(t // TM_PROJ,),
        in_specs=[
            pl.BlockSpec((TM_PROJ, D_MODEL), lambda i: (i, 0)),
            pl.BlockSpec((1, D_MODEL), lambda i: (0, 0)),
            pl.BlockSpec(w_gate.shape, lambda i: (0, 0), pipeline_mode=pl.Buffered(1)),
            pl.BlockSpec((None, TM_PROJ, PLE_DIM), lambda i: (layer, i, 0)),
            pl.BlockSpec(w_proj.shape, lambda i: (0, 0), pipeline_mode=pl.Buffered(1)),
            pl.BlockSpec((1, D_MODEL), lambda i: (0, 0)),
        ],
        out_specs=pl.BlockSpec((TM_PROJ, D_MODEL), lambda i: (i, 0)),
        out_shape=jax.ShapeDtypeStruct((t, D_MODEL), F32),
        scratch_shapes=[pltpu.VMEM((TM_PROJ, D_MODEL), BF16)],
        compiler_params=_params("parallel"),
    )(h, g_pre, w_gate, p, w_proj, g_post)


def _rope_tables(seq, rot_dim):
    inv = ROPE_THETA ** (-jnp.arange(0, rot_dim, 2, dtype=F32) / rot_dim)
    ang = jnp.arange(seq, dtype=F32)[:, None] * inv[None, :]
    return jnp.cos(ang), jnp.sin(ang)


def _lane_tables(cos, sin, tail):
    seq, half = cos.shape
    rest = LANES - 2 * half
    zeros = lambda n: jnp.zeros((seq, n), F32)
    cos_t = jnp.concatenate([cos, cos, jnp.full((seq, rest), tail, F32)], axis=1)
    sin_up = jnp.concatenate([zeros(half), sin, zeros(rest)], axis=1)
    sin_dn = jnp.concatenate([-sin, zeros(LANES - half)], axis=1)
    return cos_t, sin_up, sin_dn


def _row(v):
    return v.reshape(1, -1)


def kernel(x, p, ffn1_g_pre, ffn1_w_in, ffn1_w_out, ffn1_g_post, mix_g_pre, mix_g_post, ab_w_in, a_rel_bias, b_g_q, b_w_qup, b_g_kv, b_w_kvup, ab_w_out, c_w_in, c_lq1, c_lk1, c_lq2, c_lk2, c_g_sub, c_w_out, ffn2_g_pre, ffn2_w_in, ffn2_w_out, ffn2_g_post, ple_g_pre, ple_w_gate, ple_w_proj, ple_g_post):
    batch, seq, _ = x.shape
    t = batch * seq
    assert seq % TQ == 0 and seq % TM_IN == 0 and seq % TM_PROJ == 0 and t % TM_FFN == 0

    tab_b = _lane_tables(*_rope_tables(seq, B_ROPE_DIM), tail=0.0)
    tab_c = _lane_tables(*_rope_tables(seq, C_ROT_DIM), tail=1.0)

    ffn_f32 = [(w_in, w_out, i) for i in range(DEPTH)
               for w_in, w_out in ((ffn1_w_in, ffn1_w_out), (ffn2_w_in, ffn2_w_out))]
    ffn_f32.append(None)
    ffn_w = (ffn1_w_in[0].astype(BF16), ffn1_w_out[0].astype(BF16))
    w_mix_in = _cast_layer(ab_w_in, 0)
    pad_rope = LANES - B_ROPE_DIM
    w_q = b_w_qup.astype(BF16).reshape(-1, B_Q_LORA, HEADS, B_NOPE_DIM + B_ROPE_DIM)
    w_q = jnp.pad(w_q, ((0, 0), (0, 0), (0, 0), (0, pad_rope))).reshape(-1, B_Q_LORA, HEADS * 2 * LANES)
    b_w_kvup = b_w_kvup.astype(BF16)

    plain = (False,) * 12
    c_groups = ((True,) * 4 + (False,) * 2) * 2

    h = x.reshape(t, D_MODEL)
    p = p.reshape(DEPTH, t, PLE_DIM)
    for i in range(DEPTH):
        j = i // 2
        h, *ffn_w = _ffn(h, _row(ffn1_g_pre[i]), *ffn_w, _row(ffn1_g_post[i]), ffn_f32[2 * i + 1])
        g_mix = _row(mix_g_pre[i])
        if i % 2 == 0:
            u_a = _proj(h, g_mix, w_mix_in, 3 * A_WIDTH, tab_b, seq, plain, 0)
            w_b = jnp.pad(w_mix_in[:, 3 * A_WIDTH:], ((0, 0), (0, pad_rope)))
            q_b, k_b, v_b = _mla_proj(h, g_mix, w_b, _row(b_g_q[j]), w_q[j], _row(b_g_kv[j]),
                                      b_w_kvup[j], tab_b, seq)
            o_a, w_out, w_gate, w_emb = _band_attn(u_a, a_rel_bias[j], batch, seq,
                                                   [(ab_w_out, j), (ple_w_gate, i), (ple_w_proj, i)])
            o_b, w_mix_in = _mla_attn(q_b, k_b, v_b, batch, seq, [(c_w_in, j)])
            h = _out_proj([o_a, o_b], w_out, _row(mix_g_post[i]), h)
        else:
            lambda_init = 0.8 - 0.6 * math.exp(-0.3 * i)
            u_c = _proj(h, g_mix, w_mix_in, HEADS * C_HEAD_IN, tab_c, seq, c_groups, C_ROT_DIM // 2)
            casts = [(c_w_out, j), (ple_w_gate, i), (ple_w_proj, i)]
            if i + 1 < DEPTH:
                casts.append((ab_w_in, j + 1))
            o_c, w_out, w_gate, w_emb, *w_next = _diff_attn(
                u_c, _row(c_lq1[j]), _row(c_lk1[j]), _row(c_lq2[j]), _row(c_lk2[j]), _row(c_g_sub[j]),
                lambda_init, batch, seq, casts)
            if w_next:
                w_mix_in, = w_next
            h = _out_proj([o_c], w_out, _row(mix_g_post[i]), h)
        h, *ffn_w = _ffn(h, _row(ffn2_g_pre[i]), *ffn_w, _row(ffn2_g_post[i]), ffn_f32[2 * i + 2])
        h = _ple(h, _row(ple_g_pre[i]), w_gate, p, w_emb, _row(ple_g_post[i]), i)
    return h.reshape(batch, seq, D_MODEL)
```
